```python
import math
import jax, jax.numpy as jnp
from jax import lax
import numpy as np

D_MODEL = 2048
BATCH = 4
SEQ = 4096
DEPTH = 1

MIX_WIDTH = D_MODEL
POOL_WIDTH = MIX_WIDTH // 2
POOL_WINDOWS = (2, 4, 8, 16)
N_POOL_GROUPS = len(POOL_WINDOWS)
POOL_GROUP = POOL_WIDTH // N_POOL_GROUPS
ATTN_WIDTH = MIX_WIDTH - POOL_WIDTH
HEAD_DIM = 128
N_HEADS = ATTN_WIDTH // HEAD_DIM
N_KV_HEADS = 2
Q_PER_KV = N_HEADS // N_KV_HEADS
WINDOW = 128
BLOCK = 128
N_BUCKETS = 32
MAX_DISTANCE = 128
N_EXPERTS = 16
CAPACITY_FACTOR = 2
D_FF = 2 * D_MODEL
IN_WIDTH = POOL_WIDTH + ATTN_WIDTH + 2 * N_KV_HEADS * HEAD_DIM
EPS = 1e-6

kernel_name = "hybrid_pool_swa_ecmoe_encoder"


def rmsnorm(x, g):
    xf = x.astype(jnp.float32)
    y = xf * lax.rsqrt(jnp.mean(xf * xf, axis=-1, keepdims=True) + EPS)
    return (y * g.astype(jnp.float32)).astype(x.dtype)


def multiscale_pool(u, pool_w, pool_scale):
    B, S, _ = u.shape
    ug = u.reshape(B, S, N_POOL_GROUPS, POOL_GROUP)
    ugf = ug.astype(jnp.float32)
    c = jnp.cumsum(ugf, axis=1)
    c = jnp.pad(c, ((0, 0), (1, 0), (0, 0), (0, 0)))
    t = jnp.arange(S)
    means = []
    for g, w in enumerate(POOL_WINDOWS):
        lo = jnp.clip(t - w // 2, 0, S)
        hi = jnp.clip(t + w // 2, 0, S)
        cg = c[:, :, g]
        cnt = (hi - lo).astype(jnp.float32)[None, :, None]
        means.append((cg[:, hi] - cg[:, lo]) / cnt)
    pooled = (jnp.stack(means, axis=2) - ugf).astype(u.dtype)
    mixed = jnp.einsum('bsgc,gcd->bsgd', pooled, pool_w)
    return mixed.reshape(B, S, POOL_WIDTH) * pool_scale


def t5_bucket(rel):
    half = N_BUCKETS // 2
    max_exact = half // 2
    ret = jnp.where(rel > 0, half, 0)
    n = jnp.abs(rel)
    nf = jnp.maximum(n, 1).astype(jnp.float32)
    large = max_exact + (jnp.log(nf / max_exact) / math.log(MAX_DISTANCE / max_exact)
                         * (half - max_exact)).astype(jnp.int32)
    large = jnp.minimum(large, half - 1)
    return ret + jnp.where(n < max_exact, n, large)


def windowed_gqa(q, k, v, sink, rel_bias):
    B, S, _ = q.shape
    nb = S // BLOCK
    qb = q.reshape(B, nb, BLOCK, N_KV_HEADS, Q_PER_KV, HEAD_DIM)

    def band(t):
        t = t.reshape(B, nb, BLOCK, N_KV_HEADS, HEAD_DIM)
        tp = jnp.pad(t, ((0, 0), (1, 1), (0, 0), (0, 0), (0, 0)))
        return jnp.concatenate([tp[:, :-2], tp[:, 1:-1], tp[:, 2:]], axis=2)

    kb, vb = band(k), band(v)
    scale = HEAD_DIM ** -0.5
    logits = jnp.einsum('bnqkgd,bnskd->bnkgqs', qb, kb).astype(jnp.float32) * scale

    qi = jnp.arange(BLOCK)[:, None]
    kj = jnp.arange(3 * BLOCK)[None, :] - BLOCK
    rel = kj - qi
    bias = rel_bias[t5_bucket(rel)].astype(jnp.float32)
    bias = jnp.transpose(bias, (2, 0, 1)).reshape(N_KV_HEADS, Q_PER_KV, BLOCK, 3 * BLOCK)

    kpos = jnp.arange(nb)[:, None] * BLOCK - BLOCK + jnp.arange(3 * BLOCK)[None, :]
    valid = (kpos >= 0) & (kpos < S)
    mask = (jnp.abs(rel) <= WINDOW)[None, :, :] & valid[:, None, :]
    logits = jnp.where(mask[None, :, None, None], logits + bias, -jnp.inf)

    sink_col = jnp.broadcast_to(
        sink.astype(jnp.float32).reshape(1, 1, N_KV_HEADS, Q_PER_KV, 1, 1),
        logits.shape[:-1] + (1,))
    p = jax.nn.softmax(jnp.concatenate([logits, sink_col], axis=-1), axis=-1)[..., :-1]
    out = jnp.einsum('bnkgqs,bnskd->bnqkgd', p.astype(v.dtype), vb)
    return out.reshape(B, S, ATTN_WIDTH)


def expert_choice_moe(h, w_router, w_gate, w_up, w_down):
    B, S, D = h.shape
    cap = CAPACITY_FACTOR * S // N_EXPERTS
    aff = jax.nn.softmax(jnp.einsum('bsd,de->bse', h, w_router).astype(jnp.float32), axis=-1)
    gate, idx = lax.top_k(jnp.swapaxes(aff, 1, 2), cap)
    xs = jax.vmap(lambda hb, ib: hb[ib])(h, idx)
    a = jnp.einsum('becd,edf->becf', xs, w_gate)
    u = jnp.einsum('becd,edf->becf', xs, w_up)
    y = jnp.einsum('becf,efd->becd', jax.nn.silu(a) * u, w_down)
    y = y * gate[..., None].astype(y.dtype)
    flat = (jnp.arange(B)[:, None, None] * S + idx).reshape(-1)
    out = jnp.zeros((B * S, D), h.dtype).at[flat].add(y.reshape(-1, D))
    return out.reshape(B, S, D)


def setup_inputs(seed: int = 0) -> dict:
    key = jax.random.key(seed)
    ks = jax.random.split(key, 20)
    f32 = jnp.float32
    nrm = lambda k, shape, s: jax.random.normal(k, shape, f32) * s
    L = DEPTH
    return {
        "x": nrm(ks[0], (BATCH, SEQ, D_MODEL), 1.0),
        "norm1_g": 1.0 + nrm(ks[1], (L, D_MODEL), 0.02),
        "w_in": nrm(ks[2], (L, D_MODEL, IN_WIDTH), D_MODEL ** -0.5),
        "pool_w": nrm(ks[3], (L, N_POOL_GROUPS, POOL_GROUP, POOL_GROUP), POOL_GROUP ** -0.5),
        "pool_scale": 1.0 + nrm(ks[4], (L, POOL_WIDTH), 0.02),
        "rel_bias": nrm(ks[5], (N_BUCKETS, N_HEADS), 0.5),
        "sink": nrm(ks[6], (L, N_HEADS), 0.5),
        "gn_pool": 1.0 + nrm(ks[7], (L, POOL_WIDTH), 0.02),
        "gn_attn": 1.0 + nrm(ks[8], (L, ATTN_WIDTH), 0.02),
        "w_out": nrm(ks[9], (L, MIX_WIDTH, D_MODEL), MIX_WIDTH ** -0.5),
        "norm2_g": 1.0 + nrm(ks[10], (L, D_MODEL), 0.02),
        "w_router": nrm(ks[11], (L, D_MODEL, N_EXPERTS), D_MODEL ** -0.5),
        "w_gate": nrm(ks[12], (L, N_EXPERTS, D_MODEL, D_FF), D_MODEL ** -0.5),
        "w_up": nrm(ks[13], (L, N_EXPERTS, D_MODEL, D_FF), D_MODEL ** -0.5),
        "w_down": nrm(ks[14], (L, N_EXPERTS, D_FF, D_MODEL), D_FF ** -0.5),
        "final_g": 1.0 + nrm(ks[15], (D_MODEL,), 0.02),
    }


def reference(x, norm1_g, w_in, pool_w, pool_scale, rel_bias, sink, gn_pool, gn_attn,
              w_out, norm2_g, w_router, w_gate, w_up, w_down, final_g):
    q_end = POOL_WIDTH + ATTN_WIDTH
    k_end = q_end + N_KV_HEADS * HEAD_DIM
    for l in range(DEPTH):
        h = rmsnorm(x, norm1_g[l])
        proj = jnp.einsum('bsd,dp->bsp', h, w_in[l])
        u_pool = proj[..., :POOL_WIDTH]
        q = proj[..., POOL_WIDTH:q_end]
        k = proj[..., q_end:k_end]
        v = proj[..., k_end:]
        y_pool = rmsnorm(multiscale_pool(u_pool, pool_w[l], pool_scale[l]), gn_pool[l])
        y_attn = rmsnorm(windowed_gqa(q, k, v, sink[l], rel_bias), gn_attn[l])
        mix = jnp.concatenate([y_pool, y_attn], axis=-1)
        x = x + jnp.einsum('bsm,md->bsd', mix, w_out[l])
        h2 = rmsnorm(x, norm2_g[l])
        x = x + expert_choice_moe(h2, w_router[l], w_gate[l], w_up[l], w_down[l])
    return rmsnorm(x, final_g)
```

```python
import functools
import math

import numpy as np
import jax
import jax.numpy as jnp
from jax import lax
from jax.experimental import pallas as pl
from jax.experimental.pallas import tpu as pltpu

D_MODEL = 2048
POOL_WIDTH = 1024
POOL_WINDOWS = (2, 4, 8, 16)
POOL_GROUP = POOL_WIDTH // len(POOL_WINDOWS)
POOL_HALO = 8
ATTN_WIDTH = 1024
HEAD_DIM = 128
N_HEADS = ATTN_WIDTH // HEAD_DIM
N_KV_HEADS = 2
Q_PER_KV = N_HEADS // N_KV_HEADS
KV_WIDTH = 2 * N_KV_HEADS * HEAD_DIM
WINDOW = 128
BLOCK = 128
N_BUCKETS = 32
MAX_DISTANCE = 128
N_EXPERTS = 16
CAPACITY_FACTOR = 2
D_FF = 2 * D_MODEL
IN_WIDTH = POOL_WIDTH + ATTN_WIDTH + KV_WIDTH
EPS = 1e-6
MASKED = -1e30
LANES = 128
VMEM_LIMIT_BYTES = 56 * 1024 * 1024

F32 = jnp.float32
BF16 = jnp.bfloat16


def _params(n_axes):
    return pltpu.CompilerParams(dimension_semantics=("arbitrary",) * n_axes,
                                vmem_limit_bytes=VMEM_LIMIT_BYTES)


def _t5_bucket_table():
    half = N_BUCKETS // 2
    max_exact = half // 2
    qi = np.arange(BLOCK)[:, None]
    kj = np.arange(3 * BLOCK)[None, :] - BLOCK
    rel = kj - qi
    n = np.abs(rel)
    nf = np.maximum(n, 1).astype(np.float64)
    large = max_exact + np.floor(np.log(nf / max_exact) / math.log(MAX_DISTANCE / max_exact)
                                 * (half - max_exact) + 1e-9).astype(np.int64)
    large = np.minimum(large, half - 1)
    bucket = np.where(rel > 0, half, 0) + np.where(n < max_exact, n, large)
    return np.where(n <= WINDOW, bucket, -1).astype(np.int32)


def _bias_kernel(bucket_ref, relb_ref, o_ref):
    bk = bucket_ref[...]
    for h in range(N_HEADS):
        acc = jnp.full(bk.shape, MASKED, F32)
        for b in range(N_BUCKETS):
            acc = jnp.where(bk == b, relb_ref[b, h], acc)
        o_ref[h] = acc


def _bias_table(rel_bias):
    bucket = jnp.asarray(_t5_bucket_table())
    return pl.pallas_call(
        _bias_kernel,
        out_shape=jax.ShapeDtypeStruct((N_HEADS, BLOCK, 3 * BLOCK), F32),
        in_specs=[pl.BlockSpec(memory_space=pltpu.VMEM), pl.BlockSpec(memory_space=pltpu.SMEM)],
        out_specs=pl.BlockSpec(memory_space=pltpu.VMEM),
        name="bias_table",
    )(bucket, rel_bias)


IN_CHUNK = 512


def _inproj_kernel(x_ref, g_ref, w_ref, u_ref, q_ref, kv_ref, h_scr):
    xf = x_ref[...]
    ms = jnp.mean(xf * xf, axis=-1, keepdims=True)
    h_scr[...] = (xf * lax.rsqrt(ms + EPS) * g_ref[...]).astype(BF16)
    for n0 in range(0, IN_WIDTH, IN_CHUNK):
        acc = jnp.dot(h_scr[...], w_ref[:, n0:n0 + IN_CHUNK], preferred_element_type=F32)
        if n0 < POOL_WIDTH:
            u_ref[:, n0:n0 + IN_CHUNK] = acc
        elif n0 < POOL_WIDTH + ATTN_WIDTH:
            q_ref[:, n0 - POOL_WIDTH:n0 - POOL_WIDTH + IN_CHUNK] = acc.astype(BF16)
        else:
            kv_ref[...] = acc.astype(BF16)


def _inproj(x2d, g, w_bf16, tm):
    T = x2d.shape[0]
    return pl.pallas_call(
        _inproj_kernel,
        grid=(T // tm,),
        in_specs=[pl.BlockSpec((tm, D_MODEL), lambda i: (i, 0)),
                  pl.BlockSpec((1, D_MODEL), lambda i: (0, 0)),
                  pl.BlockSpec((D_MODEL, IN_WIDTH), lambda i: (0, 0))],
        out_specs=[pl.BlockSpec((tm, POOL_WIDTH), lambda i: (i, 0)),
                   pl.BlockSpec((tm, ATTN_WIDTH), lambda i: (i, 0)),
                   pl.BlockSpec((tm, KV_WIDTH), lambda i: (i, 0))],
        out_shape=[jax.ShapeDtypeStruct((T, POOL_WIDTH), F32),
                   jax.ShapeDtypeStruct((T, ATTN_WIDTH), BF16),
                   jax.ShapeDtypeStruct((T, KV_WIDTH), BF16)],
        scratch_shapes=[pltpu.VMEM((tm, D_MODEL), BF16)],
        compiler_params=_params(1),
        name="norm1_inproj",
    )(x2d, g, w_bf16)


def _pool_kernel(up_ref, uc_ref, un_ref, pw_ref, ps_ref, gn_ref, o_ref, ext, mix, *, seq):
    s = pl.program_id(1)
    ns = pl.num_programs(1)
    ts = uc_ref.shape[1]
    h = POOL_HALO
    ext[0:h, :] = jnp.where(s > 0, up_ref[0], 0.0)
    ext[h:h + ts, :] = uc_ref[0]
    ext[h + ts:2 * h + ts, :] = jnp.where(s < ns - 1, un_ref[0], 0.0)
    t = s * ts + lax.broadcasted_iota(jnp.int32, (ts, 1), 0)
    for g, w in enumerate(POOL_WINDOWS):
        hw = w // 2
        c0 = g * POOL_GROUP
        acc = ext[h - hw:h - hw + ts, c0:c0 + POOL_GROUP]
        for d in range(-hw + 1, hw):
            acc = acc + ext[h + d:h + d + ts, c0:c0 + POOL_GROUP]
        cnt = (jnp.minimum(t + hw, seq) - jnp.maximum(t - hw, 0)).astype(F32)
        pooled = acc / cnt - ext[h:h + ts, c0:c0 + POOL_GROUP]
        m = jnp.dot(pooled.astype(BF16), pw_ref[g], preferred_element_type=F32)
        mix[:, c0:c0 + POOL_GROUP] = m * ps_ref[:, c0:c0 + POOL_GROUP]
    mm = mix[...]
    ms = jnp.mean(mm * mm, axis=-1, keepdims=True)
    o_ref[0] = (mm * lax.rsqrt(ms + EPS) * gn_ref[...]).astype(BF16)


def _pool(u3d, pool_w_bf16, pool_scale, gn_pool, ts):
    B, S, _ = u3d.shape
    hb = ts // POOL_HALO
    n_hblk = S // POOL_HALO
    return pl.pallas_call(
        functools.partial(_pool_kernel, seq=S),
        grid=(B, S // ts),
        in_specs=[pl.BlockSpec((1, POOL_HALO, POOL_WIDTH), lambda b, s: (b, jnp.maximum(s * hb - 1, 0), 0)),
                  pl.BlockSpec((1, ts, POOL_WIDTH), lambda b, s: (b, s, 0)),
                  pl.BlockSpec((1, POOL_HALO, POOL_WIDTH),
                               lambda b, s: (b, jnp.minimum((s + 1) * hb, n_hblk - 1), 0)),
                  pl.BlockSpec((len(POOL_WINDOWS), POOL_GROUP, POOL_GROUP), lambda b, s: (0, 0, 0)),
                  pl.BlockSpec((1, POOL_WIDTH), lambda b, s: (0, 0)),
                  pl.BlockSpec((1, POOL_WIDTH), lambda b, s: (0, 0))],
        out_specs=pl.BlockSpec((1, ts, POOL_WIDTH), lambda b, s: (b, s, 0)),
        out_shape=jax.ShapeDtypeStruct((B, S, POOL_WIDTH), BF16),
        scratch_shapes=[pltpu.VMEM((ts + 2 * POOL_HALO, POOL_WIDTH), F32),
                        pltpu.VMEM((ts, POOL_WIDTH), F32)],
        compiler_params=_params(2),
        name="pool_mixer",
    )(u3d, u3d, u3d, pool_w_bf16, pool_scale, gn_pool)


def _attn_kernel(q_ref, kvp_ref, kvc_ref, kvn_ref, bias_ref, sink_ref, gn_ref, o_ref, att):
    n = pl.program_id(1)
    nb = pl.num_programs(1)
    scale = HEAD_DIM ** -0.5
    col = lax.broadcasted_iota(jnp.int32, (1, 3 * BLOCK), 1)
    col_ok = jnp.logical_and(jnp.logical_or(n > 0, col >= BLOCK),
                             jnp.logical_or(n < nb - 1, col < 2 * BLOCK))
    col_mask = jnp.where(col_ok, 0.0, MASKED)
    for k in range(N_KV_HEADS):
        ks = slice(k * HEAD_DIM, (k + 1) * HEAD_DIM)
        vs = slice((N_KV_HEADS + k) * HEAD_DIM, (N_KV_HEADS + k + 1) * HEAD_DIM)
        kk = jnp.concatenate([kvp_ref[0, :, ks], kvc_ref[0, :, ks], kvn_ref[0, :, ks]], axis=0)
        vv = jnp.concatenate([kvp_ref[0, :, vs], kvc_ref[0, :, vs], kvn_ref[0, :, vs]], axis=0)
        for g in range(Q_PER_KV):
            hd = k * Q_PER_KV + g
            hs = slice(hd * HEAD_DIM, (hd + 1) * HEAD_DIM)
            logits = lax.dot_general(q_ref[0, :, hs], kk, (((1,), (1,)), ((), ())),
                                     preferred_element_type=F32)
            logits = logits * scale + bias_ref[hd] + col_mask
            sink = sink_ref[hd]
            m = jnp.maximum(jnp.max(logits, axis=-1, keepdims=True), sink)
            p = jnp.exp(logits - m)
            denom = jnp.sum(p, axis=-1, keepdims=True) + jnp.exp(sink - m)
            o = jnp.dot(p.astype(BF16), vv, preferred_element_type=F32)
            att[:, hs] = o / denom
    a = att[...]
    ms = jnp.mean(a * a, axis=-1, keepdims=True)
    o_ref[0] = (a * lax.rsqrt(ms + EPS) * gn_ref[...]).astype(BF16)


def _attention(q3d, kv3d, bias_tbl, sink, gn_attn):
    B, S, _ = q3d.shape
    nb = S // BLOCK
    kv_spec = lambda f: pl.BlockSpec((1, BLOCK, KV_WIDTH), f)
    return pl.pallas_call(
        _attn_kernel,
        grid=(B, nb),
        in_specs=[pl.BlockSpec((1, BLOCK, ATTN_WIDTH), lambda b, n: (b, n, 0)),
                  kv_spec(lambda b, n: (b, jnp.maximum(n - 1, 0), 0)),
                  kv_spec(lambda b, n: (b, n, 0)),
                  kv_spec(lambda b, n: (b, jnp.minimum(n + 1, nb - 1), 0)),
                  pl.BlockSpec((N_HEADS, BLOCK, 3 * BLOCK), lambda b, n: (0, 0, 0)),
                  pl.BlockSpec(memory_space=pltpu.SMEM),
                  pl.BlockSpec((1, ATTN_WIDTH), lambda b, n: (0, 0))],
        out_specs=pl.BlockSpec((1, BLOCK, ATTN_WIDTH), lambda b, n: (b, n, 0)),
        out_shape=jax.ShapeDtypeStruct((B, S, ATTN_WIDTH), BF16),
        scratch_shapes=[pltpu.VMEM((BLOCK, ATTN_WIDTH), F32)],
        compiler_params=_params(2),
        name="window_attn",
    )(q3d, kv3d, kv3d, kv3d, bias_tbl, sink, gn_attn)


OUT_CHUNK = 512


def _outproj_kernel(yp_ref, ya_ref, x_ref, wo_ref, g2_ref, wrt_ref, x1_ref, h2_ref, aff_ref):
    tm = x_ref.shape[0]
    ss = jnp.zeros((tm, 1), F32)
    for n0 in range(0, D_MODEL, OUT_CHUNK):
        ns = slice(n0, n0 + OUT_CHUNK)
        acc = jnp.dot(yp_ref[...], wo_ref[0:POOL_WIDTH, ns], preferred_element_type=F32)
        acc = acc + jnp.dot(ya_ref[...], wo_ref[POOL_WIDTH:, ns], preferred_element_type=F32)
        x1 = x_ref[:, ns] + acc
        x1_ref[:, ns] = x1
        ss = ss + jnp.sum(x1 * x1, axis=-1, keepdims=True)
    inv = lax.rsqrt(ss / D_MODEL + EPS)
    h2_ref[...] = x1_ref[...] * inv * g2_ref[...]
    logits_t = lax.dot_general(wrt_ref[...], h2_ref[...], (((1,), (1,)), ((), ())),
                               precision=lax.Precision.HIGHEST, preferred_element_type=F32)
    mx = jnp.max(logits_t, axis=0, keepdims=True)
    ex = jnp.exp(logits_t - mx)
    aff_ref[0] = ex / jnp.sum(ex, axis=0, keepdims=True)


def _outproj(yp2d, ya2d, x2d, wo_bf16, g2, wr_t, tm, B, S):
    T = x2d.shape[0]
    per_b = S // tm
    return pl.pallas_call(
        _outproj_kernel,
        grid=(T // tm,),
        in_specs=[pl.BlockSpec((tm, POOL_WIDTH), lambda i: (i, 0)),
                  pl.BlockSpec((tm, ATTN_WIDTH), lambda i: (i, 0)),
                  pl.BlockSpec((tm, D_MODEL), lambda i: (i, 0)),
                  pl.BlockSpec((D_MODEL, D_MODEL), lambda i: (0, 0)),
                  pl.BlockSpec((1, D_MODEL), lambda i: (0, 0)),
                  pl.BlockSpec((N_EXPERTS, D_MODEL), lambda i: (0, 0))],
        out_specs=[pl.BlockSpec((tm, D_MODEL), lambda i: (i, 0)),
                   pl.BlockSpec((tm, D_MODEL), lambda i: (i, 0)),
                   pl.BlockSpec((1, N_EXPERTS, tm), lambda i: (i // per_b, 0, i % per_b))],
        out_shape=[jax.ShapeDtypeStruct((T, D_MODEL), F32),
                   jax.ShapeDtypeStruct((T, D_MODEL), F32),
                   jax.ShapeDtypeStruct((B, N_EXPERTS, S), F32)],
        compiler_params=_params(1),
        name="outproj_norm2_router",
    )(yp2d, ya2d, x2d, wo_bf16, g2, wr_t)


def _route_kernel(aff_ref, rows_ref, gate_ref, posm, *, cap, slot_blk):
    b = pl.program_id(0)
    S = aff_ref.shape[2]
    v = aff_ref[0]
    bits = pltpu.bitcast(v, jnp.int32)

    def count(mask):
        return jnp.sum(jnp.where(mask, 1.0, 0.0), axis=1, keepdims=True)

    thr = jnp.zeros((N_EXPERTS, 1), jnp.int32)
    for bit in range(30, -1, -1):
        cand = thr | (1 << bit)
        thr = jnp.where(count(bits >= cand) >= cap, cand, thr)
    gt = bits > thr
    eq = bits == thr
    need = cap - count(gt)
    tok = lax.broadcasted_iota(jnp.int32, (N_EXPERTS, S), 1)
    bound = jnp.zeros((N_EXPERTS, 1), jnp.int32)
    for bit in range(S.bit_length() - 1, -1, -1):
        cand = bound | (1 << bit)
        bound = jnp.where(count(jnp.logical_and(eq, tok < cand)) <= need, cand, bound)
    sel = jnp.where(jnp.logical_or(gt, jnp.logical_and(eq, tok < bound)), 1.0, 0.0)

    tri = jnp.where(lax.broadcasted_iota(jnp.int32, (LANES, LANES), 0)
                    < lax.broadcasted_iota(jnp.int32, (LANES, LANES), 1), 1.0, 0.0).astype(BF16)
    off = jnp.zeros((N_EXPERTS, 1), F32)
    for c in range(S // LANES):
        sc = sel[:, c * LANES:(c + 1) * LANES]
        within = jnp.dot(sc.astype(BF16), tri, preferred_element_type=F32)
        posm[:, c * LANES:(c + 1) * LANES] = jnp.where(sc > 0.0, within + off, -1.0)
        off = off + jnp.sum(sc, axis=1, keepdims=True)

    rows_ref[...] = jnp.zeros(rows_ref.shape, jnp.int32)
    gate_ref[...] = jnp.zeros(gate_ref.shape, F32)
    lane = lax.broadcasted_iota(jnp.int32, (1, LANES), 1)
    for e in range(N_EXPERTS):
        for jb in range(cap // slot_blk):
            slot = (lax.broadcasted_iota(jnp.int32, (slot_blk, 1), 0) + jb * slot_blk).astype(F32)

            def body(c, carry):
                acc_row, acc_gate = carry
                o = pl.multiple_of(c * LANES, LANES)
                hit = posm[e:e + 1, pl.ds(o, LANES)] == slot
                rowid = (lane + (c * LANES + b * S)).astype(F32)
                acc_row = jnp.where(hit, rowid, acc_row)
                acc_gate = jnp.where(hit, aff_ref[0, e:e + 1, pl.ds(o, LANES)], acc_gate)
                return acc_row, acc_gate

            zero = jnp.zeros((slot_blk, LANES), F32)
            acc_row, acc_gate = lax.fori_loop(0, S // LANES, body, (zero, zero))
            js = slice(jb * slot_blk, (jb + 1) * slot_blk)
            rows_ref[0, js, e:e + 1] = jnp.sum(acc_row, axis=1, keepdims=True).astype(jnp.int32)
            gate_ref[0, js, e:e + 1] = jnp.sum(acc_gate, axis=1, keepdims=True)


def _route(aff_t, cap):
    B, _, S = aff_t.shape
    slot_blk = min(LANES, cap)
    return pl.pallas_call(
        functools.partial(_route_kernel, cap=cap, slot_blk=slot_blk),
        grid=(B,),
        in_specs=[pl.BlockSpec((1, N_EXPERTS, S), lambda b: (b, 0, 0))],
        out_specs=[pl.BlockSpec((1, cap, LANES), lambda b: (b, 0, 0)),
                   pl.BlockSpec((1, cap, LANES), lambda b: (b, 0, 0))],
        out_shape=[jax.ShapeDtypeStruct((B, cap, LANES), jnp.int32),
                   jax.ShapeDtypeStruct((B, cap, LANES), F32)],
        scratch_shapes=[pltpu.VMEM((N_EXPERTS, S), F32)],
        compiler_params=_params(1),
        name="expert_choice_route",
    )(aff_t)


FF_TILE = 256


def _ffn_kernel(rows_ref, gate_ref, h2_hbm, wg_ref, wu_ref, wd_ref, xin_hbm, out_hbm,
                xs, yacc, gcol, stage, gsem, ssem, *, n_rows, chunk, m_chunk):
    del xin_hbm
    e = pl.program_id(0)
    f = pl.program_id(1)
    nf = pl.num_programs(1)
    n_chunks = n_rows // chunk

    def row_copy(src_hbm, c, r, slot):
        row = rows_ref[0, 0, c * chunk + r]
        return pltpu.make_async_copy(src_hbm.at[pl.ds(row, 1)], stage.at[slot, pl.ds(r, 1)], gsem.at[slot])

    def start_gather(src_hbm, c, slot):
        def body(r, carry):
            row_copy(src_hbm, c, r, slot).start()
            return carry
        lax.fori_loop(0, chunk, body, 0, unroll=8)

    def wait_gather(src_hbm, slot):
        pltpu.make_async_copy(src_hbm.at[pl.ds(0, chunk)], stage.at[slot], gsem.at[slot]).wait()

    @pl.when(f == 0)
    def _gather_rows():
        start_gather(h2_hbm, 0, 0)
        for c in range(n_chunks):
            if c + 1 < n_chunks:
                start_gather(h2_hbm, c + 1, (c + 1) % 2)
            wait_gather(h2_hbm, c % 2)
            xs[c * chunk:(c + 1) * chunk, :] = stage[c % 2].astype(BF16)
        yacc[...] = jnp.zeros(yacc.shape, F32)

    wg = wg_ref[0].astype(BF16)
    wu = wu_ref[0].astype(BF16)
    wd = wd_ref[0].astype(BF16)
    for m in range(n_rows // m_chunk):
        ms = slice(m * m_chunk, (m + 1) * m_chunk)
        xm = xs[ms, :]
        a = jnp.dot(xm, wg, preferred_element_type=F32)
        u = jnp.dot(xm, wu, preferred_element_type=F32)
        hm = (a * (1.0 / (1.0 + jnp.exp(-a))) * u).astype(BF16)
        yacc[ms, :] += jnp.dot(hm, wd, preferred_element_type=F32)

    @pl.when(f == nf - 1)
    def _scatter_rows():
        lane = lax.broadcasted_iota(jnp.int32, (1, 1, LANES), 2)
        g = jnp.sum(jnp.where(lane == e, gate_ref[...], 0.0), axis=-1, keepdims=True)
        gcol[...] = g.reshape(n_rows, 1)

        def start_scatter(c):
            def body(r, carry):
                row = rows_ref[0, 0, c * chunk + r]
                pltpu.make_async_copy(yacc.at[pl.ds(c * chunk + r, 1)], out_hbm.at[pl.ds(row, 1)], ssem).start()
                return carry
            lax.fori_loop(0, chunk, body, 0, unroll=8)

        start_gather(out_hbm, 0, 0)
        for c in range(n_chunks):
            if c + 1 < n_chunks:
                start_gather(out_hbm, c + 1, (c + 1) % 2)
            wait_gather(out_hbm, c % 2)
            cs = slice(c * chunk, (c + 1) * chunk)
            yacc[cs, :] = stage[c % 2] + gcol[cs, :] * yacc[cs, :]
            start_scatter(c)
        for c in range(n_chunks):
            pltpu.make_async_copy(yacc.at[pl.ds(0, chunk)], out_hbm.at[pl.ds(0, chunk)], ssem).wait()


def _expert_ffn(rows, gate_col, h2, w_gate, w_up, w_down, x1):
    T = x1.shape[0]
    n_rows = rows.shape[-1]
    B, cap, _ = gate_col.shape
    chunk = min(256, n_rows)
    m_chunk = min(512, n_rows)
    return pl.pallas_call(
        functools.partial(_ffn_kernel, n_rows=n_rows, chunk=chunk, m_chunk=m_chunk),
        grid=(N_EXPERTS, D_FF // FF_TILE),
        in_specs=[pl.BlockSpec((1, 1, n_rows), lambda e, f: (e, 0, 0), memory_space=pltpu.SMEM),
                  pl.BlockSpec((B, cap, LANES), lambda e, f: (0, 0, 0)),
                  pl.BlockSpec(memory_space=pl.ANY),
                  pl.BlockSpec((1, D_MODEL, FF_TILE), lambda e, f: (e, 0, f)),
                  pl.BlockSpec((1, D_MODEL, FF_TILE), lambda e, f: (e, 0, f)),
                  pl.BlockSpec((1, FF_TILE, D_MODEL), lambda e, f: (e, f, 0)),
                  pl.BlockSpec(memory_space=pl.ANY)],
        out_specs=pl.BlockSpec(memory_space=pl.ANY),
        out_shape=jax.ShapeDtypeStruct((T, D_MODEL), F32),
        input_output_aliases={6: 0},
        scratch_shapes=[pltpu.VMEM((n_rows, D_MODEL), BF16),
                        pltpu.VMEM((n_rows, D_MODEL), F32),
                        pltpu.VMEM((n_rows, 1), F32),
                        pltpu.VMEM((2, chunk, D_MODEL), F32),
                        pltpu.SemaphoreType.DMA((2,)),
                        pltpu.SemaphoreType.DMA(())],
        compiler_params=_params(2),
        name="expert_ffn",
    )(rows, gate_col, h2, w_gate, w_up, w_down, x1)


def _final_norm_kernel(x_ref, g_ref, o_ref):
    xf = x_ref[...]
    ms = jnp.mean(xf * xf, axis=-1, keepdims=True)
    o_ref[...] = xf * lax.rsqrt(ms + EPS) * g_ref[...]


def _final_norm(x2d, g, tm):
    T = x2d.shape[0]
    return pl.pallas_call(
        _final_norm_kernel,
        grid=(T // tm,),
        in_specs=[pl.BlockSpec((tm, D_MODEL), lambda i: (i, 0)),
                  pl.BlockSpec((1, D_MODEL), lambda i: (0, 0))],
        out_specs=pl.BlockSpec((tm, D_MODEL), lambda i: (i, 0)),
        out_shape=jax.ShapeDtypeStruct((T, D_MODEL), F32),
        compiler_params=_params(1),
        name="final_norm",
    )(x2d, g)


def kernel(x, norm1_g, w_in, pool_w, pool_scale, rel_bias, sink, gn_pool, gn_attn, w_out, norm2_g,
           w_router, w_gate, w_up, w_down, final_g):
    B, S, D = x.shape
    assert D == D_MODEL and S % BLOCK == 0 and norm1_g.shape[0] == 1
    T = B * S
    cap = CAPACITY_FACTOR * S // N_EXPERTS
    tm = min(512, S)
    row = lambda a: a.reshape(1, -1)

    bias_tbl = _bias_table(rel_bias)
    x2d = x.reshape(T, D)
    u, q, kv = _inproj(x2d, row(norm1_g[0]), w_in[0].astype(BF16), tm)
    y_pool = _pool(u.reshape(B, S, POOL_WIDTH), pool_w[0].astype(BF16), row(pool_scale[0]),
                   row(gn_pool[0]), tm)
    y_attn = _attention(q.reshape(B, S, ATTN_WIDTH), kv.reshape(B, S, KV_WIDTH), bias_tbl, sink[0],
                        row(gn_attn[0]))
    x1, h2, aff_t = _outproj(y_pool.reshape(T, POOL_WIDTH), y_attn.reshape(T, ATTN_WIDTH), x2d,
                             w_out[0].astype(BF16), row(norm2_g[0]), w_router[0].T, tm, B, S)
    rows_col, gate_col = _route(aff_t, cap)
    rows = rows_col[:, :, :N_EXPERTS].transpose(2, 0, 1).reshape(N_EXPERTS, 1, B * cap)
    x2 = _expert_ffn(rows, gate_col, h2, w_gate[0], w_up[0], w_down[0], x1)
    return _final_norm(x2, row(final_g), tm).reshape(B, S, D)
```

```python
import functools
import math

import numpy as np
import jax
import jax.numpy as jnp
from jax import lax
from jax.experimental import pallas as pl
from jax.experimental.pallas import tpu as pltpu

D_MODEL = 2048
POOL_WIDTH = 1024
POOL_WINDOWS = (2, 4, 8, 16)
POOL_GROUP = POOL_WIDTH // len(POOL_WINDOWS)
POOL_HALO = 8
ATTN_WIDTH = 1024
HEAD_DIM = 128
N_HEADS = ATTN_WIDTH // HEAD_DIM
N_KV_HEADS = 2
Q_PER_KV = N_HEADS // N_KV_HEADS
KV_WIDTH = 2 * N_KV_HEADS * HEAD_DIM
WINDOW = 128
BLOCK = 128
N_BUCKETS = 32
MAX_DISTANCE = 128
N_EXPERTS = 16
CAPACITY_FACTOR = 2
D_FF = 2 * D_MODEL
IN_WIDTH = POOL_WIDTH + ATTN_WIDTH + KV_WIDTH
EPS = 1e-6
MASKED = -1e30
LANES = 128
VMEM_LIMIT_BYTES = 56 * 1024 * 1024

F32 = jnp.float32
BF16 = jnp.bfloat16


def _params(n_axes):
    return pltpu.CompilerParams(dimension_semantics=("arbitrary",) * n_axes,
                                vmem_limit_bytes=VMEM_LIMIT_BYTES)


def _t5_bucket_table():
    half = N_BUCKETS // 2
    max_exact = half // 2
    qi = np.arange(BLOCK)[:, None]
    kj = np.arange(3 * BLOCK)[None, :] - BLOCK
    rel = kj - qi
    n = np.abs(rel)
    nf = np.maximum(n, 1).astype(np.float64)
    large = max_exact + np.floor(np.log(nf / max_exact) / math.log(MAX_DISTANCE / max_exact)
                                 * (half - max_exact) + 1e-9).astype(np.int64)
    large = np.minimum(large, half - 1)
    bucket = np.where(rel > 0, half, 0) + np.where(n < max_exact, n, large)
    return np.where(n <= WINDOW, bucket, -1).astype(np.int32)


def _bias_kernel(bucket_ref, relb_ref, o_ref):
    bk = bucket_ref[...]
    for h in range(N_HEADS):
        acc = jnp.full(bk.shape, MASKED, F32)
        for b in range(N_BUCKETS):
            acc = jnp.where(bk == b, relb_ref[b, h], acc)
        o_ref[h] = acc


def _bias_table(rel_bias):
    bucket = jnp.asarray(_t5_bucket_table())
    return pl.pallas_call(
        _bias_kernel,
        out_shape=jax.ShapeDtypeStruct((N_HEADS, BLOCK, 3 * BLOCK), F32),
        in_specs=[pl.BlockSpec(memory_space=pltpu.VMEM), pl.BlockSpec(memory_space=pltpu.SMEM)],
        out_specs=pl.BlockSpec(memory_space=pltpu.VMEM),
        name="bias_table",
    )(bucket, rel_bias)


IN_CHUNK = 512


def _inproj_kernel(x_ref, g_ref, w_ref, u_ref, q_ref, kv_ref, h_scr):
    xf = x_ref[...]
    ms = jnp.mean(xf * xf, axis=-1, keepdims=True)
    h_scr[...] = (xf * lax.rsqrt(ms + EPS) * g_ref[...]).astype(BF16)
    for n0 in range(0, IN_WIDTH, IN_CHUNK):
        acc = jnp.dot(h_scr[...], w_ref[:, n0:n0 + IN_CHUNK], preferred_element_type=F32)
        if n0 < POOL_WIDTH:
            u_ref[:, n0:n0 + IN_CHUNK] = acc
        elif n0 < POOL_WIDTH + ATTN_WIDTH:
            q_ref[:, n0 - POOL_WIDTH:n0 - POOL_WIDTH + IN_CHUNK] = acc.astype(BF16)
        else:
            kv_ref[...] = acc.astype(BF16)


def _inproj(x2d, g, w_bf16, tm):
    T = x2d.shape[0]
    return pl.pallas_call(
        _inproj_kernel,
        grid=(T // tm,),
        in_specs=[pl.BlockSpec((tm, D_MODEL), lambda i: (i, 0)),
                  pl.BlockSpec((1, D_MODEL), lambda i: (0, 0)),
                  pl.BlockSpec((D_MODEL, IN_WIDTH), lambda i: (0, 0))],
        out_specs=[pl.BlockSpec((tm, POOL_WIDTH), lambda i: (i, 0)),
                   pl.BlockSpec((tm, ATTN_WIDTH), lambda i: (i, 0)),
                   pl.BlockSpec((tm, KV_WIDTH), lambda i: (i, 0))],
        out_shape=[jax.ShapeDtypeStruct((T, POOL_WIDTH), F32),
                   jax.ShapeDtypeStruct((T, ATTN_WIDTH), BF16),
                   jax.ShapeDtypeStruct((T, KV_WIDTH), BF16)],
        scratch_shapes=[pltpu.VMEM((tm, D_MODEL), BF16)],
        compiler_params=_params(1),
        name="norm1_inproj",
    )(x2d, g, w_bf16)


def _pool_kernel(up_ref, uc_ref, un_ref, pw_ref, ps_ref, gn_ref, o_ref, ext, mix, *, seq):
    s = pl.program_id(1)
    ns = pl.num_programs(1)
    ts = uc_ref.shape[1]
    h = POOL_HALO
    ext[0:h, :] = jnp.where(s > 0, up_ref[0], 0.0)
    ext[h:h + ts, :] = uc_ref[0]
    ext[h + ts:2 * h + ts, :] = jnp.where(s < ns - 1, un_ref[0], 0.0)
    t = s * ts + lax.broadcasted_iota(jnp.int32, (ts, 1), 0)
    for g, w in enumerate(POOL_WINDOWS):
        hw = w // 2
        c0 = g * POOL_GROUP
        acc = ext[h - hw:h - hw + ts, c0:c0 + POOL_GROUP]
        for d in range(-hw + 1, hw):
            acc = acc + ext[h + d:h + d + ts, c0:c0 + POOL_GROUP]
        cnt = (jnp.minimum(t + hw, seq) - jnp.maximum(t - hw, 0)).astype(F32)
        pooled = acc / cnt - ext[h:h + ts, c0:c0 + POOL_GROUP]
        m = jnp.dot(pooled.astype(BF16), pw_ref[g], preferred_element_type=F32)
        mix[:, c0:c0 + POOL_GROUP] = m * ps_ref[:, c0:c0 + POOL_GROUP]
    mm = mix[...]
    ms = jnp.mean(mm * mm, axis=-1, keepdims=True)
    o_ref[0] = (mm * lax.rsqrt(ms + EPS) * gn_ref[...]).astype(BF16)


def _pool(u3d, pool_w_bf16, pool_scale, gn_pool, ts):
    B, S, _ = u3d.shape
    hb = ts // POOL_HALO
    n_hblk = S // POOL_HALO
    return pl.pallas_call(
        functools.partial(_pool_kernel, seq=S),
        grid=(B, S // ts),
        in_specs=[pl.BlockSpec((1, POOL_HALO, POOL_WIDTH), lambda b, s: (b, jnp.maximum(s * hb - 1, 0), 0)),
                  pl.BlockSpec((1, ts, POOL_WIDTH), lambda b, s: (b, s, 0)),
                  pl.BlockSpec((1, POOL_HALO, POOL_WIDTH),
                               lambda b, s: (b, jnp.minimum((s + 1) * hb, n_hblk - 1), 0)),
                  pl.BlockSpec((len(POOL_WINDOWS), POOL_GROUP, POOL_GROUP), lambda b, s: (0, 0, 0)),
                  pl.BlockSpec((1, POOL_WIDTH), lambda b, s: (0, 0)),
                  pl.BlockSpec((1, POOL_WIDTH), lambda b, s: (0, 0))],
        out_specs=pl.BlockSpec((1, ts, POOL_WIDTH), lambda b, s: (b, s, 0)),
        out_shape=jax.ShapeDtypeStruct((B, S, POOL_WIDTH), BF16),
        scratch_shapes=[pltpu.VMEM((ts + 2 * POOL_HALO, POOL_WIDTH), F32),
                        pltpu.VMEM((ts, POOL_WIDTH), F32)],
        compiler_params=_params(2),
        name="pool_mixer",
    )(u3d, u3d, u3d, pool_w_bf16, pool_scale, gn_pool)


def _attn_kernel(q_ref, kvp_ref, kvc_ref, kvn_ref, bias_ref, sink_ref, gn_ref, o_ref, att):
    n = pl.program_id(1)
    nb = pl.num_programs(1)
    scale = HEAD_DIM ** -0.5
    col = lax.broadcasted_iota(jnp.int32, (1, 3 * BLOCK), 1)
    col_ok = jnp.logical_and(jnp.logical_or(n > 0, col >= BLOCK),
                             jnp.logical_or(n < nb - 1, col < 2 * BLOCK))
    col_mask = jnp.where(col_ok, 0.0, MASKED)
    for k in range(N_KV_HEADS):
        ks = slice(k * HEAD_DIM, (k + 1) * HEAD_DIM)
        vs = slice((N_KV_HEADS + k) * HEAD_DIM, (N_KV_HEADS + k + 1) * HEAD_DIM)
        kk = jnp.concatenate([kvp_ref[0, :, ks], kvc_ref[0, :, ks], kvn_ref[0, :, ks]], axis=0)
        vv = jnp.concatenate([kvp_ref[0, :, vs], kvc_ref[0, :, vs], kvn_ref[0, :, vs]], axis=0)
        h0 = k * Q_PER_KV
        qk = jnp.concatenate([q_ref[0, :, (h0 + g) * HEAD_DIM:(h0 + g + 1) * HEAD_DIM]
                              for g in range(Q_PER_KV)], axis=0)
        logits = lax.dot_general(qk, kk, (((1,), (1,)), ((), ())), preferred_element_type=F32)
        bias = bias_ref[h0:h0 + Q_PER_KV].reshape(Q_PER_KV * BLOCK, 3 * BLOCK)
        logits = logits * scale + bias + col_mask
        head = lax.broadcasted_iota(jnp.int32, (Q_PER_KV * BLOCK, 1), 0) // BLOCK
        sink = jnp.full((Q_PER_KV * BLOCK, 1), sink_ref[h0], F32)
        for g in range(1, Q_PER_KV):
            sink = jnp.where(head == g, sink_ref[h0 + g], sink)
        m = jnp.maximum(jnp.max(logits, axis=-1, keepdims=True), sink)
        p = jnp.exp(logits - m)
        denom = jnp.sum(p, axis=-1, keepdims=True) + jnp.exp(sink - m)
        o = jnp.dot(p.astype(BF16), vv, preferred_element_type=F32) / denom
        for g in range(Q_PER_KV):
            att[:, (h0 + g) * HEAD_DIM:(h0 + g + 1) * HEAD_DIM] = o[g * BLOCK:(g + 1) * BLOCK]
    a = att[...]
    ms = jnp.mean(a * a, axis=-1, keepdims=True)
    o_ref[0] = (a * lax.rsqrt(ms + EPS) * gn_ref[...]).astype(BF16)


def _attention(q3d, kv3d, bias_tbl, sink, gn_attn):
    B, S, _ = q3d.shape
    nb = S // BLOCK
    kv_spec = lambda f: pl.BlockSpec((1, BLOCK, KV_WIDTH), f)
    return pl.pallas_call(
        _attn_kernel,
        grid=(B, nb),
        in_specs=[pl.BlockSpec((1, BLOCK, ATTN_WIDTH), lambda b, n: (b, n, 0)),
                  kv_spec(lambda b, n: (b, jnp.maximum(n - 1, 0), 0)),
                  kv_spec(lambda b, n: (b, n, 0)),
                  kv_spec(lambda b, n: (b, jnp.minimum(n + 1, nb - 1), 0)),
                  pl.BlockSpec((N_HEADS, BLOCK, 3 * BLOCK), lambda b, n: (0, 0, 0)),
                  pl.BlockSpec(memory_space=pltpu.SMEM),
                  pl.BlockSpec((1, ATTN_WIDTH), lambda b, n: (0, 0))],
        out_specs=pl.BlockSpec((1, BLOCK, ATTN_WIDTH), lambda b, n: (b, n, 0)),
        out_shape=jax.ShapeDtypeStruct((B, S, ATTN_WIDTH), BF16),
        scratch_shapes=[pltpu.VMEM((BLOCK, ATTN_WIDTH), F32)],
        compiler_params=_params(2),
        name="window_attn",
    )(q3d, kv3d, kv3d, kv3d, bias_tbl, sink, gn_attn)


OUT_CHUNK = 512


def _outproj_kernel(yp_ref, ya_ref, x_ref, wo_ref, g2_ref, wrt_ref, x1_ref, h2_ref, aff_ref, mix):
    tm = x_ref.shape[0]
    mix[:, 0:POOL_WIDTH] = yp_ref[...]
    mix[:, POOL_WIDTH:] = ya_ref[...]
    ss = jnp.zeros((tm, 1), F32)
    for n0 in range(0, D_MODEL, OUT_CHUNK):
        ns = slice(n0, n0 + OUT_CHUNK)
        x1 = x_ref[:, ns] + jnp.dot(mix[...], wo_ref[:, ns], preferred_element_type=F32)
        x1_ref[:, ns] = x1
        ss = ss + jnp.sum(x1 * x1, axis=-1, keepdims=True)
    inv = lax.rsqrt(ss / D_MODEL + EPS)
    h2 = x1_ref[...] * inv * g2_ref[...]
    h2_ref[...] = h2
    logits_t = lax.dot_general(wrt_ref[...], h2.astype(BF16), (((1,), (1,)), ((), ())),
                               preferred_element_type=F32)
    mx = jnp.max(logits_t, axis=0, keepdims=True)
    ex = jnp.exp(logits_t - mx)
    aff_ref[0] = ex / jnp.sum(ex, axis=0, keepdims=True)


def _outproj(yp2d, ya2d, x2d, wo_bf16, g2, wr_t, tm, B, S):
    T = x2d.shape[0]
    per_b = S // tm
    return pl.pallas_call(
        _outproj_kernel,
        grid=(T // tm,),
        in_specs=[pl.BlockSpec((tm, POOL_WIDTH), lambda i: (i, 0)),
                  pl.BlockSpec((tm, ATTN_WIDTH), lambda i: (i, 0)),
                  pl.BlockSpec((tm, D_MODEL), lambda i: (i, 0)),
                  pl.BlockSpec((D_MODEL, D_MODEL), lambda i: (0, 0)),
                  pl.BlockSpec((1, D_MODEL), lambda i: (0, 0)),
                  pl.BlockSpec((N_EXPERTS, D_MODEL), lambda i: (0, 0))],
        out_specs=[pl.BlockSpec((tm, D_MODEL), lambda i: (i, 0)),
                   pl.BlockSpec((tm, D_MODEL), lambda i: (i, 0)),
                   pl.BlockSpec((1, N_EXPERTS, tm), lambda i: (i // per_b, 0, i % per_b))],
        out_shape=[jax.ShapeDtypeStruct((T, D_MODEL), F32),
                   jax.ShapeDtypeStruct((T, D_MODEL), F32),
                   jax.ShapeDtypeStruct((B, N_EXPERTS, S), F32)],
        scratch_shapes=[pltpu.VMEM((tm, POOL_WIDTH + ATTN_WIDTH), BF16)],
        compiler_params=_params(1),
        name="outproj_norm2_router",
    )(yp2d, ya2d, x2d, wo_bf16, g2, wr_t)


def _route_kernel(aff_ref, rows_ref, gate_ref, posm, *, cap, slot_blk):
    b = pl.program_id(0)
    S = aff_ref.shape[2]
    v = aff_ref[0]
    bits = pltpu.bitcast(v, jnp.int32)

    def count(mask):
        return jnp.sum(jnp.where(mask, 1.0, 0.0), axis=1, keepdims=True)

    thr = jnp.zeros((N_EXPERTS, 1), jnp.int32)
    for bit in range(30, -1, -1):
        cand = thr | (1 << bit)
        thr = jnp.where(count(bits >= cand) >= cap, cand, thr)
    gt = bits > thr
    eq = bits == thr
    need = cap - count(gt)
    tok = lax.broadcasted_iota(jnp.int32, (N_EXPERTS, S), 1)
    bound = jnp.zeros((N_EXPERTS, 1), jnp.int32)
    for bit in range(S.bit_length() - 1, -1, -1):
        cand = bound | (1 << bit)
        bound = jnp.where(count(jnp.logical_and(eq, tok < cand)) <= need, cand, bound)
    sel = jnp.where(jnp.logical_or(gt, jnp.logical_and(eq, tok < bound)), 1.0, 0.0)

    tri = jnp.where(lax.broadcasted_iota(jnp.int32, (LANES, LANES), 0)
                    < lax.broadcasted_iota(jnp.int32, (LANES, LANES), 1), 1.0, 0.0).astype(BF16)
    off = jnp.zeros((N_EXPERTS, 1), F32)
    for c in range(S // LANES):
        sc = sel[:, c * LANES:(c + 1) * LANES]
        within = jnp.dot(sc.astype(BF16), tri, preferred_element_type=F32)
        posm[:, c * LANES:(c + 1) * LANES] = jnp.where(sc > 0.0, within + off, -1.0)
        off = off + jnp.sum(sc, axis=1, keepdims=True)

    rows_ref[...] = jnp.zeros(rows_ref.shape, jnp.int32)
    gate_ref[...] = jnp.zeros(gate_ref.shape, F32)
    lane = lax.broadcasted_iota(jnp.int32, (1, LANES), 1)
    for e in range(N_EXPERTS):
        for jb in range(cap // slot_blk):
            slot = (lax.broadcasted_iota(jnp.int32, (slot_blk, 1), 0) + jb * slot_blk).astype(F32)

            def body(c, carry):
                acc_row, acc_gate = carry
                o = pl.multiple_of(c * LANES, LANES)
                hit = posm[e:e + 1, pl.ds(o, LANES)] == slot
                rowid = (lane + (c * LANES + b * S)).astype(F32)
                acc_row = jnp.where(hit, rowid, acc_row)
                acc_gate = jnp.where(hit, aff_ref[0, e:e + 1, pl.ds(o, LANES)], acc_gate)
                return acc_row, acc_gate

            zero = jnp.zeros((slot_blk, LANES), F32)
            acc_row, acc_gate = lax.fori_loop(0, S // LANES, body, (zero, zero))
            js = slice(jb * slot_blk, (jb + 1) * slot_blk)
            rows_ref[0, js, e:e + 1] = jnp.sum(acc_row, axis=1, keepdims=True).astype(jnp.int32)
            gate_ref[0, js, e:e + 1] = jnp.sum(acc_gate, axis=1, keepdims=True)


def _route(aff_t, cap):
    B, _, S = aff_t.shape
    slot_blk = min(LANES, cap)
    return pl.pallas_call(
        functools.partial(_route_kernel, cap=cap, slot_blk=slot_blk),
        grid=(B,),
        in_specs=[pl.BlockSpec((1, N_EXPERTS, S), lambda b: (b, 0, 0))],
        out_specs=[pl.BlockSpec((1, cap, LANES), lambda b: (b, 0, 0)),
                   pl.BlockSpec((1, cap, LANES), lambda b: (b, 0, 0))],
        out_shape=[jax.ShapeDtypeStruct((B, cap, LANES), jnp.int32),
                   jax.ShapeDtypeStruct((B, cap, LANES), F32)],
        scratch_shapes=[pltpu.VMEM((N_EXPERTS, S), F32)],
        compiler_params=_params(1),
        name="expert_choice_route",
    )(aff_t)


FF_TILE = 256
N_FF_STEPS = D_FF // FF_TILE


def _ffn_kernel(rows_ref, rows_nxt_ref, gate_ref, h2_hbm, wg_ref, wu_ref, wd_ref, xin_hbm, out_hbm,
                xs, yacc, gcol, stage_x, stage_b, sem_x, sem_b, sem_s, *, n_rows, m_chunk):
    del xin_hbm
    e = pl.program_id(0)
    f = pl.program_id(1)
    ne = pl.num_programs(0)
    nf = N_FF_STEPS
    chunk = n_rows // nf
    n_m = n_rows // m_chunk
    per_m = chunk // n_m
    cur = lax.rem(e, 2)

    def x_copy(rows_smem, c, r, slot):
        row = rows_smem[0, 0, c * chunk + r]
        return pltpu.make_async_copy(h2_hbm.at[pl.ds(row, 1)], stage_x.at[slot, pl.ds(r, 1)], sem_x.at[slot])

    def b_copy(rows_smem, c, r, slot):
        row = rows_smem[0, 0, c * chunk + r]
        return pltpu.make_async_copy(out_hbm.at[pl.ds(row, 1)], stage_b.at[slot, pl.ds(r, 1)], sem_b.at[slot])

    def s_copy(r):
        row = rows_ref[0, 0, r]
        return pltpu.make_async_copy(yacc.at[pl.ds(r, 1)], out_hbm.at[pl.ds(row, 1)], sem_s)

    def wait_x(slot):
        pltpu.make_async_copy(h2_hbm.at[pl.ds(0, chunk)], stage_x.at[slot], sem_x.at[slot]).wait()

    def wait_b(slot):
        pltpu.make_async_copy(out_hbm.at[pl.ds(0, chunk)], stage_b.at[slot], sem_b.at[slot]).wait()

    def wait_s(n):
        pltpu.make_async_copy(yacc.at[pl.ds(0, n)], out_hbm.at[pl.ds(0, n)], sem_s).wait()

    def loop_start(copy_fn, n):
        def body(r, carry):
            copy_fn(r).start()
            return carry
        lax.fori_loop(0, n, body, 0, unroll=8)

    @pl.when(jnp.logical_and(e == 0, f == 0))
    def _first_expert_rows():
        loop_start(lambda r: x_copy(rows_ref, 0, r, 0), chunk)
        for c in range(nf - 1):
            loop_start(lambda r, c=c: x_copy(rows_ref, c + 1, r, (c + 1) % 2), chunk)
            wait_x(c % 2)
            xs[0, c * chunk:(c + 1) * chunk, :] = stage_x[c % 2].astype(BF16)
        loop_start(lambda r: b_copy(rows_ref, 0, r, 0), chunk)

    @pl.when(f == 0)
    def _per_expert():
        lane = lax.broadcasted_iota(jnp.int32, (1, 1, LANES), 2)
        g = jnp.sum(jnp.where(lane == e, gate_ref[...], 0.0), axis=-1, keepdims=True)
        gcol[...] = g.reshape(n_rows, 1)
        yacc[...] = jnp.zeros(yacc.shape, F32)

    def step(last):
        pc = lax.rem(f + (nf - 1), nf)
        pslot = lax.rem(pc, 2)
        tgt = lax.rem(e + jnp.where(f > 0, 1, 0), 2)
        wait_x(pslot)
        xs[tgt, pl.ds(pl.multiple_of(pc * chunk, chunk), chunk), :] = stage_x[pslot].astype(BF16)
        bslot = lax.rem(f, 2)
        wait_b(bslot)
        yacc[pl.ds(pl.multiple_of(f * chunk, chunk), chunk), :] += stage_b[bslot]

        wg = wg_ref[0].astype(BF16)
        wu = wu_ref[0].astype(BF16)
        wd = wd_ref[0].astype(BF16)
        xslot = lax.rem(f, 2)
        nslot = lax.rem(f + 1, 2)
        for m in range(n_m):
            for r in range(m * per_m, (m + 1) * per_m):
                x_copy(rows_nxt_ref, f, r, xslot).start()
                if not last:
                    b_copy(rows_ref, f + 1, r, nslot).start()
            ms = slice(m * m_chunk, (m + 1) * m_chunk)
            xm = xs[cur, ms, :]
            a = jnp.dot(xm, wg, preferred_element_type=F32)
            u = jnp.dot(xm, wu, preferred_element_type=F32)
            hm = (a * (1.0 / (1.0 + jnp.exp(-a))) * u).astype(BF16)
            yacc[ms, :] += gcol[ms, :] * jnp.dot(hm, wd, preferred_element_type=F32)
            if last:
                for r in range(m * m_chunk, (m + 1) * m_chunk):
                    s_copy(r).start()
        if last:
            for m in range(n_m):
                wait_s(m_chunk)

            @pl.when(e < ne - 1)
            def _next_expert_first_residual_chunk():
                loop_start(lambda r: b_copy(rows_nxt_ref, 0, r, 0), chunk)

            @pl.when(e == ne - 1)
            def _drain_unused_prefetch():
                wait_x((nf - 1) % 2)

    @pl.when(f < nf - 1)
    def _():
        step(False)

    @pl.when(f == nf - 1)
    def _():
        step(True)


def _expert_ffn(rows, gate_col, h2, w_gate, w_up, w_down, x1):
    T = x1.shape[0]
    n_rows = rows.shape[-1]
    B, cap, _ = gate_col.shape
    chunk = n_rows // N_FF_STEPS
    m_chunk = min(512, n_rows)
    assert chunk % 8 == 0 and chunk % (n_rows // m_chunk) == 0
    rows_spec = lambda f: pl.BlockSpec((1, 1, n_rows), f, memory_space=pltpu.SMEM)
    return pl.pallas_call(
        functools.partial(_ffn_kernel, n_rows=n_rows, m_chunk=m_chunk),
        grid=(N_EXPERTS, N_FF_STEPS),
        in_specs=[rows_spec(lambda e, f: (e, 0, 0)),
                  rows_spec(lambda e, f: (jnp.minimum(e + 1, N_EXPERTS - 1), 0, 0)),
                  pl.BlockSpec((B, cap, LANES), lambda e, f: (0, 0, 0)),
                  pl.BlockSpec(memory_space=pl.ANY),
                  pl.BlockSpec((1, D_MODEL, FF_TILE), lambda e, f: (e, 0, f)),
                  pl.BlockSpec((1, D_MODEL, FF_TILE), lambda e, f: (e, 0, f)),
                  pl.BlockSpec((1, FF_TILE, D_MODEL), lambda e, f: (e, f, 0)),
                  pl.BlockSpec(memory_space=pl.ANY)],
        out_specs=pl.BlockSpec(memory_space=pl.ANY),
        out_shape=jax.ShapeDtypeStruct((T, D_MODEL), F32),
        input_output_aliases={7: 0},
        scratch_shapes=[pltpu.VMEM((2, n_rows, D_MODEL), BF16),
                        pltpu.VMEM((n_rows, D_MODEL), F32),
                        pltpu.VMEM((n_rows, 1), F32),
                        pltpu.VMEM((2, chunk, D_MODEL), F32),
                        pltpu.VMEM((2, chunk, D_MODEL), F32),
                        pltpu.SemaphoreType.DMA((2,)),
                        pltpu.SemaphoreType.DMA((2,)),
                        pltpu.SemaphoreType.DMA(())],
        compiler_params=_params(2),
        name="expert_ffn",
    )(rows, rows, gate_col, h2, w_gate, w_up, w_down, x1)


def _final_norm_kernel(x_ref, g_ref, o_ref):
    xf = x_ref[...]
    ms = jnp.mean(xf * xf, axis=-1, keepdims=True)
    o_ref[...] = xf * lax.rsqrt(ms + EPS) * g_ref[...]


def _final_norm(x2d, g, tm):
    T = x2d.shape[0]
    return pl.pallas_call(
        _final_norm_kernel,
        grid=(T // tm,),
        in_specs=[pl.BlockSpec((tm, D_MODEL), lambda i: (i, 0)),
                  pl.BlockSpec((1, D_MODEL), lambda i: (0, 0))],
        out_specs=pl.BlockSpec((tm, D_MODEL), lambda i: (i, 0)),
        out_shape=jax.ShapeDtypeStruct((T, D_MODEL), F32),
        compiler_params=_params(1),
        name="final_norm",
    )(x2d, g)


def kernel(x, norm1_g, w_in, pool_w, pool_scale, rel_bias, sink, gn_pool, gn_attn, w_out, norm2_g,
           w_router, w_gate, w_up, w_down, final_g):
    B, S, D = x.shape
    assert D == D_MODEL and S % BLOCK == 0 and norm1_g.shape[0] == 1
    T = B * S
    cap = CAPACITY_FACTOR * S // N_EXPERTS
    tm = min(512, S)
    row = lambda a: a.reshape(1, -1)

    bias_tbl = _bias_table(rel_bias)
    x2d = x.reshape(T, D)
    u, q, kv = _inproj(x2d, row(norm1_g[0]), w_in[0].astype(BF16), tm)
    y_pool = _pool(u.reshape(B, S, POOL_WIDTH), pool_w[0].astype(BF16), row(pool_scale[0]),
                   row(gn_pool[0]), tm)
    y_attn = _attention(q.reshape(B, S, ATTN_WIDTH), kv.reshape(B, S, KV_WIDTH), bias_tbl, sink[0],
                        row(gn_attn[0]))
    x1, h2, aff_t = _outproj(y_pool.reshape(T, POOL_WIDTH), y_attn.reshape(T, ATTN_WIDTH), x2d,
                             w_out[0].astype(BF16), row(norm2_g[0]), w_router[0].T.astype(BF16), tm, B, S)
    rows_col, gate_col = _route(aff_t, cap)
    rows = rows_col[:, :, :N_EXPERTS].transpose(2, 0, 1).reshape(N_EXPERTS, 1, B * cap)
    x2 = _expert_ffn(rows, gate_col, h2, w_gate[0], w_up[0], w_down[0], x1)
    return _final_norm(x2, row(final_g), tm).reshape(B, S, D)
```

```python
import functools
import math

import numpy as np
import jax
import jax.numpy as jnp
from jax import lax
from jax.experimental import pallas as pl
from jax.experimental.pallas import tpu as pltpu

D_MODEL = 2048
POOL_WIDTH = 1024
POOL_WINDOWS = (2, 4, 8, 16)
POOL_GROUP = POOL_WIDTH // len(POOL_WINDOWS)
POOL_HALO = 8
ATTN_WIDTH = 1024
HEAD_DIM = 128
N_HEADS = ATTN_WIDTH // HEAD_DIM
N_KV_HEADS = 2
Q_PER_KV = N_HEADS // N_KV_HEADS
KV_WIDTH = 2 * N_KV_HEADS * HEAD_DIM
WINDOW = 128
BLOCK = 128
N_BUCKETS = 32
MAX_DISTANCE = 128
N_EXPERTS = 16
CAPACITY_FACTOR = 2
D_FF = 2 * D_MODEL
IN_WIDTH = POOL_WIDTH + ATTN_WIDTH + KV_WIDTH
EPS = 1e-6
MASKED = -1e30
LANES = 128
VMEM_LIMIT_BYTES = 56 * 1024 * 1024

F32 = jnp.float32
BF16 = jnp.bfloat16


def _params(n_axes):
    return pltpu.CompilerParams(dimension_semantics=("arbitrary",) * n_axes,
                                vmem_limit_bytes=VMEM_LIMIT_BYTES)


def _t5_bucket_table():
    half = N_BUCKETS // 2
    max_exact = half // 2
    qi = np.arange(BLOCK)[:, None]
    kj = np.arange(3 * BLOCK)[None, :] - BLOCK
    rel = kj - qi
    n = np.abs(rel)
    nf = np.maximum(n, 1).astype(np.float64)
    large = max_exact + np.floor(np.log(nf / max_exact) / math.log(MAX_DISTANCE / max_exact)
                                 * (half - max_exact) + 1e-9).astype(np.int64)
    large = np.minimum(large, half - 1)
    bucket = np.where(rel > 0, half, 0) + np.where(n < max_exact, n, large)
    return np.where(n <= WINDOW, bucket, -1).astype(np.int32)


def _bias_kernel(bucket_ref, relb_ref, o_ref):
    bk = bucket_ref[...]
    for h in range(N_HEADS):
        acc = jnp.full(bk.shape, MASKED, F32)
        for b in range(N_BUCKETS):
            acc = jnp.where(bk == b, relb_ref[b, h], acc)
        o_ref[h] = acc


def _bias_table(rel_bias):
    bucket = jnp.asarray(_t5_bucket_table())
    return pl.pallas_call(
        _bias_kernel,
        out_shape=jax.ShapeDtypeStruct((N_HEADS, BLOCK, 3 * BLOCK), F32),
        in_specs=[pl.BlockSpec(memory_space=pltpu.VMEM), pl.BlockSpec(memory_space=pltpu.SMEM)],
        out_specs=pl.BlockSpec(memory_space=pltpu.VMEM),
        name="bias_table",
    )(bucket, rel_bias)


IN_CHUNK = 512


def _inproj_kernel(x_ref, g_ref, w_ref, u_ref, q_ref, kv_ref, h_scr):
    xf = x_ref[...]
    ms = jnp.mean(xf * xf, axis=-1, keepdims=True)
    h_scr[...] = (xf * lax.rsqrt(ms + EPS) * g_ref[...]).astype(BF16)
    for n0 in range(0, IN_WIDTH, IN_CHUNK):
        acc = jnp.dot(h_scr[...], w_ref[:, n0:n0 + IN_CHUNK], preferred_element_type=F32)
        if n0 < POOL_WIDTH:
            u_ref[:, n0:n0 + IN_CHUNK] = acc
        elif n0 < POOL_WIDTH + ATTN_WIDTH:
            q_ref[:, n0 - POOL_WIDTH:n0 - POOL_WIDTH + IN_CHUNK] = acc.astype(BF16)
        else:
            kv_ref[...] = acc.astype(BF16)


def _inproj(x2d, g, w_bf16, tm):
    T = x2d.shape[0]
    return pl.pallas_call(
        _inproj_kernel,
        grid=(T // tm,),
        in_specs=[pl.BlockSpec((tm, D_MODEL), lambda i: (i, 0)),
                  pl.BlockSpec((1, D_MODEL), lambda i: (0, 0)),
                  pl.BlockSpec((D_MODEL, IN_WIDTH), lambda i: (0, 0))],
        out_specs=[pl.BlockSpec((tm, POOL_WIDTH), lambda i: (i, 0)),
                   pl.BlockSpec((tm, ATTN_WIDTH), lambda i: (i, 0)),
                   pl.BlockSpec((tm, KV_WIDTH), lambda i: (i, 0))],
        out_shape=[jax.ShapeDtypeStruct((T, POOL_WIDTH), F32),
                   jax.ShapeDtypeStruct((T, ATTN_WIDTH), BF16),
                   jax.ShapeDtypeStruct((T, KV_WIDTH), BF16)],
        scratch_shapes=[pltpu.VMEM((tm, D_MODEL), BF16)],
        compiler_params=_params(1),
        name="norm1_inproj",
    )(x2d, g, w_bf16)


def _pool_kernel(up_ref, uc_ref, un_ref, pw_ref, ps_ref, gn_ref, o_ref, ext, mix, *, seq):
    s = pl.program_id(1)
    ns = pl.num_programs(1)
    ts = uc_ref.shape[1]
    h = POOL_HALO
    ext[0:h, :] = jnp.where(s > 0, up_ref[0], 0.0)
    ext[h:h + ts, :] = uc_ref[0]
    ext[h + ts:2 * h + ts, :] = jnp.where(s < ns - 1, un_ref[0], 0.0)
    t = s * ts + lax.broadcasted_iota(jnp.int32, (ts, 1), 0)
    for g, w in enumerate(POOL_WINDOWS):
        hw = w // 2
        c0 = g * POOL_GROUP
        acc = ext[h - hw:h - hw + ts, c0:c0 + POOL_GROUP]
        for d in range(-hw + 1, hw):
            acc = acc + ext[h + d:h + d + ts, c0:c0 + POOL_GROUP]
        cnt = (jnp.minimum(t + hw, seq) - jnp.maximum(t - hw, 0)).astype(F32)
        pooled = acc / cnt - ext[h:h + ts, c0:c0 + POOL_GROUP]
        m = jnp.dot(pooled.astype(BF16), pw_ref[g], preferred_element_type=F32)
        mix[:, c0:c0 + POOL_GROUP] = m * ps_ref[:, c0:c0 + POOL_GROUP]
    mm = mix[...]
    ms = jnp.mean(mm * mm, axis=-1, keepdims=True)
    o_ref[0] = (mm * lax.rsqrt(ms + EPS) * gn_ref[...]).astype(BF16)


def _pool(u3d, pool_w_bf16, pool_scale, gn_pool, ts):
    B, S, _ = u3d.shape
    hb = ts // POOL_HALO
    n_hblk = S // POOL_HALO
    return pl.pallas_call(
        functools.partial(_pool_kernel, seq=S),
        grid=(B, S // ts),
        in_specs=[pl.BlockSpec((1, POOL_HALO, POOL_WIDTH), lambda b, s: (b, jnp.maximum(s * hb - 1, 0), 0)),
                  pl.BlockSpec((1, ts, POOL_WIDTH), lambda b, s: (b, s, 0)),
                  pl.BlockSpec((1, POOL_HALO, POOL_WIDTH),
                               lambda b, s: (b, jnp.minimum((s + 1) * hb, n_hblk - 1), 0)),
                  pl.BlockSpec((len(POOL_WINDOWS), POOL_GROUP, POOL_GROUP), lambda b, s: (0, 0, 0)),
                  pl.BlockSpec((1, POOL_WIDTH), lambda b, s: (0, 0)),
                  pl.BlockSpec((1, POOL_WIDTH), lambda b, s: (0, 0))],
        out_specs=pl.BlockSpec((1, ts, POOL_WIDTH), lambda b, s: (b, s, 0)),
        out_shape=jax.ShapeDtypeStruct((B, S, POOL_WIDTH), BF16),
        scratch_shapes=[pltpu.VMEM((ts + 2 * POOL_HALO, POOL_WIDTH), F32),
                        pltpu.VMEM((ts, POOL_WIDTH), F32)],
        compiler_params=_params(2),
        name="pool_mixer",
    )(u3d, u3d, u3d, pool_w_bf16, pool_scale, gn_pool)


def _attn_kernel(q_ref, kvp_ref, kvc_ref, kvn_ref, bias_ref, sink_ref, gn_ref, o_ref, att):
    n = pl.program_id(1)
    nb = pl.num_programs(1)
    scale = HEAD_DIM ** -0.5
    col = lax.broadcasted_iota(jnp.int32, (1, 3 * BLOCK), 1)
    col_ok = jnp.logical_and(jnp.logical_or(n > 0, col >= BLOCK),
                             jnp.logical_or(n < nb - 1, col < 2 * BLOCK))
    col_mask = jnp.where(col_ok, 0.0, MASKED)
    for k in range(N_KV_HEADS):
        ks = slice(k * HEAD_DIM, (k + 1) * HEAD_DIM)
        vs = slice((N_KV_HEADS + k) * HEAD_DIM, (N_KV_HEADS + k + 1) * HEAD_DIM)
        kk = jnp.concatenate([kvp_ref[0, :, ks], kvc_ref[0, :, ks], kvn_ref[0, :, ks]], axis=0)
        vv = jnp.concatenate([kvp_ref[0, :, vs], kvc_ref[0, :, vs], kvn_ref[0, :, vs]], axis=0)
        h0 = k * Q_PER_KV
        qk = jnp.concatenate([q_ref[0, :, (h0 + g) * HEAD_DIM:(h0 + g + 1) * HEAD_DIM]
                              for g in range(Q_PER_KV)], axis=0)
        logits = lax.dot_general(qk, kk, (((1,), (1,)), ((), ())), preferred_element_type=F32)
        bias = bias_ref[h0:h0 + Q_PER_KV].reshape(Q_PER_KV * BLOCK, 3 * BLOCK)
        logits = logits * scale + bias + col_mask
        head = lax.broadcasted_iota(jnp.int32, (Q_PER_KV * BLOCK, 1), 0) // BLOCK
        sink = jnp.full((Q_PER_KV * BLOCK, 1), sink_ref[h0], F32)
        for g in range(1, Q_PER_KV):
            sink = jnp.where(head == g, sink_ref[h0 + g], sink)
        m = jnp.maximum(jnp.max(logits, axis=-1, keepdims=True), sink)
        p = jnp.exp(logits - m)
        denom = jnp.sum(p, axis=-1, keepdims=True) + jnp.exp(sink - m)
        o = jnp.dot(p.astype(BF16), vv, preferred_element_type=F32) / denom
        for g in range(Q_PER_KV):
            att[:, (h0 + g) * HEAD_DIM:(h0 + g + 1) * HEAD_DIM] = o[g * BLOCK:(g + 1) * BLOCK]
    a = att[...]
    ms = jnp.mean(a * a, axis=-1, keepdims=True)
    o_ref[0] = (a * lax.rsqrt(ms + EPS) * gn_ref[...]).astype(BF16)


def _attention(q3d, kv3d, bias_tbl, sink, gn_attn):
    B, S, _ = q3d.shape
    nb = S // BLOCK
    kv_spec = lambda f: pl.BlockSpec((1, BLOCK, KV_WIDTH), f)
    return pl.pallas_call(
        _attn_kernel,
        grid=(B, nb),
        in_specs=[pl.BlockSpec((1, BLOCK, ATTN_WIDTH), lambda b, n: (b, n, 0)),
                  kv_spec(lambda b, n: (b, jnp.maximum(n - 1, 0), 0)),
                  kv_spec(lambda b, n: (b, n, 0)),
                  kv_spec(lambda b, n: (b, jnp.minimum(n + 1, nb - 1), 0)),
                  pl.BlockSpec((N_HEADS, BLOCK, 3 * BLOCK), lambda b, n: (0, 0, 0)),
                  pl.BlockSpec(memory_space=pltpu.SMEM),
                  pl.BlockSpec((1, ATTN_WIDTH), lambda b, n: (0, 0))],
        out_specs=pl.BlockSpec((1, BLOCK, ATTN_WIDTH), lambda b, n: (b, n, 0)),
        out_shape=jax.ShapeDtypeStruct((B, S, ATTN_WIDTH), BF16),
        scratch_shapes=[pltpu.VMEM((BLOCK, ATTN_WIDTH), F32)],
        compiler_params=_params(2),
        name="window_attn",
    )(q3d, kv3d, kv3d, kv3d, bias_tbl, sink, gn_attn)


OUT_CHUNK = 512


def _outproj_kernel(yp_ref, ya_ref, x_ref, wo_ref, g2_ref, wrt_ref, x1_ref, h2_ref, aff_ref, mix):
    tm = x_ref.shape[0]
    mix[:, 0:POOL_WIDTH] = yp_ref[...]
    mix[:, POOL_WIDTH:] = ya_ref[...]
    ss = jnp.zeros((tm, 1), F32)
    for n0 in range(0, D_MODEL, OUT_CHUNK):
        ns = slice(n0, n0 + OUT_CHUNK)
        x1 = x_ref[:, ns] + jnp.dot(mix[...], wo_ref[:, ns], preferred_element_type=F32)
        x1_ref[:, ns] = x1
        ss = ss + jnp.sum(x1 * x1, axis=-1, keepdims=True)
    inv = lax.rsqrt(ss / D_MODEL + EPS)
    h2 = x1_ref[...] * inv * g2_ref[...]
    h2_ref[...] = h2
    logits_t = lax.dot_general(wrt_ref[...], h2.astype(BF16), (((1,), (1,)), ((), ())),
                               preferred_element_type=F32)
    mx = jnp.max(logits_t, axis=0, keepdims=True)
    ex = jnp.exp(logits_t - mx)
    aff_ref[0] = ex / jnp.sum(ex, axis=0, keepdims=True)


def _outproj(yp2d, ya2d, x2d, wo_bf16, g2, wr_t, tm, B, S):
    T = x2d.shape[0]
    per_b = S // tm
    return pl.pallas_call(
        _outproj_kernel,
        grid=(T // tm,),
        in_specs=[pl.BlockSpec((tm, POOL_WIDTH), lambda i: (i, 0)),
                  pl.BlockSpec((tm, ATTN_WIDTH), lambda i: (i, 0)),
                  pl.BlockSpec((tm, D_MODEL), lambda i: (i, 0)),
                  pl.BlockSpec((D_MODEL, D_MODEL), lambda i: (0, 0)),
                  pl.BlockSpec((1, D_MODEL), lambda i: (0, 0)),
                  pl.BlockSpec((N_EXPERTS, D_MODEL), lambda i: (0, 0))],
        out_specs=[pl.BlockSpec((tm, D_MODEL), lambda i: (i, 0)),
                   pl.BlockSpec((tm, D_MODEL), lambda i: (i, 0)),
                   pl.BlockSpec((1, N_EXPERTS, tm), lambda i: (i // per_b, 0, i % per_b))],
        out_shape=[jax.ShapeDtypeStruct((T, D_MODEL), F32),
                   jax.ShapeDtypeStruct((T, D_MODEL), F32),
                   jax.ShapeDtypeStruct((B, N_EXPERTS, S), F32)],
        scratch_shapes=[pltpu.VMEM((tm, POOL_WIDTH + ATTN_WIDTH), BF16)],
        compiler_params=_params(1),
        name="outproj_norm2_router",
    )(yp2d, ya2d, x2d, wo_bf16, g2, wr_t)


def _route_kernel(aff_ref, rows_ref, gate_ref, posm, bounds_vmem, bounds_smem, bounds_sem, *, cap, slot_blk):
    b = pl.program_id(0)
    S = aff_ref.shape[2]
    v = aff_ref[0]
    bits = pltpu.bitcast(v, jnp.int32)

    def count(mask):
        return jnp.sum(jnp.where(mask, 1.0, 0.0), axis=1, keepdims=True)

    thr = jnp.zeros((N_EXPERTS, 1), jnp.int32)
    for bit in range(30, -1, -1):
        cand = thr | (1 << bit)
        thr = jnp.where(count(bits >= cand) >= cap, cand, thr)
    gt = bits > thr
    eq = bits == thr
    need = cap - count(gt)
    tok = lax.broadcasted_iota(jnp.int32, (N_EXPERTS, S), 1)
    bound = jnp.zeros((N_EXPERTS, 1), jnp.int32)
    for bit in range(S.bit_length() - 1, -1, -1):
        cand = bound | (1 << bit)
        bound = jnp.where(count(jnp.logical_and(eq, tok < cand)) <= need, cand, bound)
    sel = jnp.where(jnp.logical_or(gt, jnp.logical_and(eq, tok < bound)), 1.0, 0.0)

    tri = jnp.where(lax.broadcasted_iota(jnp.int32, (LANES, LANES), 0)
                    < lax.broadcasted_iota(jnp.int32, (LANES, LANES), 1), 1.0, 0.0).astype(BF16)
    lane = lax.broadcasted_iota(jnp.int32, (1, LANES), 1)
    n_chunks = S // LANES
    assert n_chunks <= LANES and 2 * (cap // slot_blk) <= LANES
    off = jnp.zeros((N_EXPERTS, 1), F32)
    slots_before = jnp.full((N_EXPERTS, LANES), float(S), F32)
    slots_through = jnp.full((N_EXPERTS, LANES), float(S), F32)
    for c in range(n_chunks):
        sc = sel[:, c * LANES:(c + 1) * LANES]
        within = jnp.dot(sc.astype(BF16), tri, preferred_element_type=F32)
        posm[:, c * LANES:(c + 1) * LANES] = jnp.where(sc > 0.0, within + off, -1.0)
        slots_before = jnp.where(lane == c, off, slots_before)
        off = off + jnp.sum(sc, axis=1, keepdims=True)
        slots_through = jnp.where(lane == c, off, slots_through)

    bounds = jnp.zeros((N_EXPERTS, LANES), jnp.int32)
    for jb in range(cap // slot_blk):
        first = count(slots_through <= float(jb * slot_blk)).astype(jnp.int32)
        last = count(slots_before < float((jb + 1) * slot_blk)).astype(jnp.int32)
        bounds = jnp.where(lane == 2 * jb, first, bounds)
        bounds = jnp.where(lane == 2 * jb + 1, last, bounds)
    bounds_vmem[...] = bounds
    to_smem = pltpu.make_async_copy(bounds_vmem, bounds_smem, bounds_sem)
    to_smem.start()
    to_smem.wait()

    rows_ref[...] = jnp.zeros(rows_ref.shape, jnp.int32)
    gate_ref[...] = jnp.zeros(gate_ref.shape, F32)
    for e in range(N_EXPERTS):
        for jb in range(cap // slot_blk):
            slot = (lax.broadcasted_iota(jnp.int32, (slot_blk, 1), 0) + jb * slot_blk).astype(F32)

            def body(c, carry):
                acc_row, acc_gate = carry
                o = pl.multiple_of(c * LANES, LANES)
                hit = posm[e:e + 1, pl.ds(o, LANES)] == slot
                rowid = (lane + (c * LANES + b * S)).astype(F32)
                acc_row = jnp.where(hit, rowid, acc_row)
                acc_gate = jnp.where(hit, aff_ref[0, e:e + 1, pl.ds(o, LANES)], acc_gate)
                return acc_row, acc_gate

            zero = jnp.zeros((slot_blk, LANES), F32)
            acc_row, acc_gate = lax.fori_loop(bounds_smem[e, 2 * jb], bounds_smem[e, 2 * jb + 1], body,
                                              (zero, zero))
            js = slice(jb * slot_blk, (jb + 1) * slot_blk)
            rows_ref[0, js, e:e + 1] = jnp.sum(acc_row, axis=1, keepdims=True).astype(jnp.int32)
            gate_ref[0, js, e:e + 1] = jnp.sum(acc_gate, axis=1, keepdims=True)


def _route(aff_t, cap):
    B, _, S = aff_t.shape
    slot_blk = min(LANES, cap)
    return pl.pallas_call(
        functools.partial(_route_kernel, cap=cap, slot_blk=slot_blk),
        grid=(B,),
        in_specs=[pl.BlockSpec((1, N_EXPERTS, S), lambda b: (b, 0, 0))],
        out_specs=[pl.BlockSpec((1, cap, LANES), lambda b: (b, 0, 0)),
                   pl.BlockSpec((1, cap, LANES), lambda b: (b, 0, 0))],
        out_shape=[jax.ShapeDtypeStruct((B, cap, LANES), jnp.int32),
                   jax.ShapeDtypeStruct((B, cap, LANES), F32)],
        scratch_shapes=[pltpu.VMEM((N_EXPERTS, S), F32),
                        pltpu.VMEM((N_EXPERTS, LANES), jnp.int32),
                        pltpu.SMEM((N_EXPERTS, LANES), jnp.int32),
                        pltpu.SemaphoreType.DMA(())],
        compiler_params=_params(1),
        name="expert_choice_route",
    )(aff_t)


FF_TILE = 256
N_FF_STEPS = D_FF // FF_TILE


def _ffn_kernel(rows_ref, rows_nxt_ref, gate_ref, h2_hbm, wg_ref, wu_ref, wd_ref, xin_hbm, out_hbm,
                xs, yacc, gcol, stage_x, stage_b, sem_x, sem_b, sem_s, *, n_rows, m_chunk):
    del xin_hbm
    e = pl.program_id(0)
    f = pl.program_id(1)
    ne = pl.num_programs(0)
    nf = N_FF_STEPS
    chunk = n_rows // nf
    n_m = n_rows // m_chunk
    per_m = chunk // n_m
    cur = lax.rem(e, 2)

    def x_copy(rows_smem, c, r):
        row = rows_smem[0, 0, c * chunk + r]
        return pltpu.make_async_copy(h2_hbm.at[pl.ds(row, 1)], stage_x.at[pl.ds(r, 1)], sem_x)

    def b_copy(rows_smem, c, r):
        row = rows_smem[0, 0, c * chunk + r]
        return pltpu.make_async_copy(out_hbm.at[pl.ds(row, 1)], stage_b.at[pl.ds(r, 1)], sem_b)

    def s_copy(r):
        row = rows_ref[0, 0, r]
        return pltpu.make_async_copy(yacc.at[pl.ds(r, 1)], out_hbm.at[pl.ds(row, 1)], sem_s)

    def wait_x():
        pltpu.make_async_copy(h2_hbm.at[pl.ds(0, chunk)], stage_x, sem_x).wait()

    def wait_b():
        pltpu.make_async_copy(out_hbm.at[pl.ds(0, chunk)], stage_b, sem_b).wait()

    def wait_s(n):
        pltpu.make_async_copy(yacc.at[pl.ds(0, n)], out_hbm.at[pl.ds(0, n)], sem_s).wait()

    def loop_start(copy_fn, n):
        def body(r, carry):
            copy_fn(r).start()
            return carry
        lax.fori_loop(0, n, body, 0, unroll=8)

    @pl.when(jnp.logical_and(e == 0, f == 0))
    def _first_expert_rows():
        for c in range(nf - 1):
            loop_start(lambda r, c=c: x_copy(rows_ref, c, r), chunk)
            wait_x()
            xs[0, c * chunk:(c + 1) * chunk, :] = stage_x[...].astype(BF16)
        loop_start(lambda r: x_copy(rows_ref, nf - 1, r), chunk)
        loop_start(lambda r: b_copy(rows_ref, 0, r), chunk)

    @pl.when(f == 0)
    def _per_expert():
        lane = lax.broadcasted_iota(jnp.int32, (1, 1, LANES), 2)
        g = jnp.sum(jnp.where(lane == e, gate_ref[...], 0.0), axis=-1, keepdims=True)
        gcol[...] = g.reshape(n_rows, 1)
        yacc[...] = jnp.zeros(yacc.shape, F32)

    def step(last):
        pc = lax.rem(f + (nf - 1), nf)
        tgt = lax.rem(e + jnp.where(f > 0, 1, 0), 2)
        wait_x()
        xs[tgt, pl.ds(pl.multiple_of(pc * chunk, chunk), chunk), :] = stage_x[...].astype(BF16)
        wait_b()
        yacc[pl.ds(pl.multiple_of(f * chunk, chunk), chunk), :] += stage_b[...]

        wg = wg_ref[0].astype(BF16)
        wu = wu_ref[0].astype(BF16)
        wd = wd_ref[0].astype(BF16)
        for m in range(n_m):
            for r in range(m * per_m, (m + 1) * per_m):
                x_copy(rows_nxt_ref, f, r).start()
                if not last:
                    b_copy(rows_ref, f + 1, r).start()
            ms = slice(m * m_chunk, (m + 1) * m_chunk)
            xm = xs[cur, ms, :]
            a = jnp.dot(xm, wg, preferred_element_type=F32)
            u = jnp.dot(xm, wu, preferred_element_type=F32)
            hm = (a * (1.0 / (1.0 + jnp.exp(-a))) * u).astype(BF16)
            yacc[ms, :] += gcol[ms, :] * jnp.dot(hm, wd, preferred_element_type=F32)
            if last:
                for r in range(m * m_chunk, (m + 1) * m_chunk):
                    s_copy(r).start()
        if last:
            for m in range(n_m):
                wait_s(m_chunk)

            @pl.when(e < ne - 1)
            def _next_expert_first_residual_chunk():
                loop_start(lambda r: b_copy(rows_nxt_ref, 0, r), chunk)

            @pl.when(e == ne - 1)
            def _drain_unused_prefetch():
                wait_x()

    @pl.when(f < nf - 1)
    def _():
        step(False)

    @pl.when(f == nf - 1)
    def _():
        step(True)


def _expert_ffn(rows, gate_col, h2, w_gate, w_up, w_down, x1):
    T = x1.shape[0]
    n_rows = rows.shape[-1]
    B, cap, _ = gate_col.shape
    chunk = n_rows // N_FF_STEPS
    m_chunk = min(512, n_rows)
    assert chunk % 8 == 0 and chunk % (n_rows // m_chunk) == 0
    rows_spec = lambda f: pl.BlockSpec((1, 1, n_rows), f, memory_space=pltpu.SMEM)
    return pl.pallas_call(
        functools.partial(_ffn_kernel, n_rows=n_rows, m_chunk=m_chunk),
        grid=(N_EXPERTS, N_FF_STEPS),
        in_specs=[rows_spec(lambda e, f: (e, 0, 0)),
                  rows_spec(lambda e, f: (jnp.minimum(e + 1, N_EXPERTS - 1), 0, 0)),
                  pl.BlockSpec((B, cap, LANES), lambda e, f: (0, 0, 0)),
                  pl.BlockSpec(memory_space=pl.ANY),
                  pl.BlockSpec((1, D_MODEL, FF_TILE), lambda e, f: (e, 0, f)),
                  pl.BlockSpec((1, D_MODEL, FF_TILE), lambda e, f: (e, 0, f)),
                  pl.BlockSpec((1, FF_TILE, D_MODEL), lambda e, f: (e, f, 0)),
                  pl.BlockSpec(memory_space=pl.ANY)],
        out_specs=pl.BlockSpec(memory_space=pl.ANY),
        out_shape=jax.ShapeDtypeStruct((T, D_MODEL), F32),
        input_output_aliases={7: 0},
        scratch_shapes=[pltpu.VMEM((2, n_rows, D_MODEL), BF16),
                        pltpu.VMEM((n_rows, D_MODEL), F32),
                        pltpu.VMEM((n_rows, 1), F32),
                        pltpu.VMEM((chunk, D_MODEL), F32),
                        pltpu.VMEM((chunk, D_MODEL), F32),
                        pltpu.SemaphoreType.DMA(()),
                        pltpu.SemaphoreType.DMA(()),
                        pltpu.SemaphoreType.DMA(())],
        compiler_params=_params(2),
        name="expert_ffn",
    )(rows, rows, gate_col, h2, w_gate, w_up, w_down, x1)


def _final_norm_kernel(x_ref, g_ref, o_ref):
    xf = x_ref[...]
    ms = jnp.mean(xf * xf, axis=-1, keepdims=True)
    o_ref[...] = xf * lax.rsqrt(ms + EPS) * g_ref[...]


def _final_norm(x2d, g, tm):
    T = x2d.shape[0]
    return pl.pallas_call(
        _final_norm_kernel,
        grid=(T // tm,),
        in_specs=[pl.BlockSpec((tm, D_MODEL), lambda i: (i, 0)),
                  pl.BlockSpec((1, D_MODEL), lambda i: (0, 0))],
        out_specs=pl.BlockSpec((tm, D_MODEL), lambda i: (i, 0)),
        out_shape=jax.ShapeDtypeStruct((T, D_MODEL), F32),
        compiler_params=_params(1),
        name="final_norm",
    )(x2d, g)


def kernel(x, norm1_g, w_in, pool_w, pool_scale, rel_bias, sink, gn_pool, gn_attn, w_out, norm2_g,
           w_router, w_gate, w_up, w_down, final_g):
    B, S, D = x.shape
    assert D == D_MODEL and S % BLOCK == 0 and norm1_g.shape[0] == 1
    T = B * S
    cap = CAPACITY_FACTOR * S // N_EXPERTS
    tm = min(512, S)
    row = lambda a: a.reshape(1, -1)

    bias_tbl = _bias_table(rel_bias)
    x2d = x.reshape(T, D)
    u, q, kv = _inproj(x2d, row(norm1_g[0]), w_in[0].astype(BF16), tm)
    y_pool = _pool(u.reshape(B, S, POOL_WIDTH), pool_w[0].astype(BF16), row(pool_scale[0]),
                   row(gn_pool[0]), tm)
    y_attn = _attention(q.reshape(B, S, ATTN_WIDTH), kv.reshape(B, S, KV_WIDTH), bias_tbl, sink[0],
                        row(gn_attn[0]))
    x1, h2, aff_t = _outproj(y_pool.reshape(T, POOL_WIDTH), y_attn.reshape(T, ATTN_WIDTH), x2d,
                             w_out[0].astype(BF16), row(norm2_g[0]), w_router[0].T.astype(BF16), tm, B, S)
    rows_col, gate_col = _route(aff_t, cap)
    rows = rows_col[:, :, :N_EXPERTS].transpose(2, 0, 1).reshape(N_EXPERTS, 1, B * cap)
    x2 = _expert_ffn(rows, gate_col, h2, w_gate[0], w_up[0], w_down[0], x1)
    return _final_norm(x2, row(final_g), tm).reshape(B, S, D)
```

```python
import functools
import math

import numpy as np
import jax
import jax.numpy as jnp
from jax import lax
from jax.experimental import pallas as pl
from jax.experimental.pallas import tpu as pltpu

D_MODEL = 2048
POOL_WIDTH = 1024
POOL_WINDOWS = (2, 4, 8, 16)
POOL_GROUP = POOL_WIDTH // len(POOL_WINDOWS)
POOL_HALO = 8
ATTN_WIDTH = 1024
HEAD_DIM = 128
N_HEADS = ATTN_WIDTH // HEAD_DIM
N_KV_HEADS = 2
Q_PER_KV = N_HEADS // N_KV_HEADS
KV_WIDTH = 2 * N_KV_HEADS * HEAD_DIM
WINDOW = 128
BLOCK = 128
N_BUCKETS = 32
MAX_DISTANCE = 128
N_EXPERTS = 16
CAPACITY_FACTOR = 2
D_FF = 2 * D_MODEL
IN_WIDTH = POOL_WIDTH + ATTN_WIDTH + KV_WIDTH
EPS = 1e-6
MASKED = -1e30
LANES = 128
VMEM_LIMIT_BYTES = 56 * 1024 * 1024

F32 = jnp.float32
BF16 = jnp.bfloat16


def _params(n_axes):
    return pltpu.CompilerParams(dimension_semantics=("arbitrary",) * n_axes,
                                vmem_limit_bytes=VMEM_LIMIT_BYTES)


def _t5_bucket_table():
    half = N_BUCKETS // 2
    max_exact = half // 2
    qi = np.arange(BLOCK)[:, None]
    kj = np.arange(3 * BLOCK)[None, :] - BLOCK
    rel = kj - qi
    n = np.abs(rel)
    nf = np.maximum(n, 1).astype(np.float64)
    large = max_exact + np.floor(np.log(nf / max_exact) / math.log(MAX_DISTANCE / max_exact)
                                 * (half - max_exact) + 1e-9).astype(np.int64)
    large = np.minimum(large, half - 1)
    bucket = np.where(rel > 0, half, 0) + np.where(n < max_exact, n, large)
    return np.where(n <= WINDOW, bucket, -1).astype(np.int32)


def _bias_kernel(bucket_ref, relb_ref, o_ref):
    bk = bucket_ref[...]
    for h in range(N_HEADS):
        acc = jnp.full(bk.shape, MASKED, F32)
        for b in range(N_BUCKETS):
            acc = jnp.where(bk == b, relb_ref[b, h], acc)
        o_ref[h] = acc


def _bias_table(rel_bias):
    bucket = jnp.asarray(_t5_bucket_table())
    return pl.pallas_call(
        _bias_kernel,
        out_shape=jax.ShapeDtypeStruct((N_HEADS, BLOCK, 3 * BLOCK), F32),
        in_specs=[pl.BlockSpec(memory_space=pltpu.VMEM), pl.BlockSpec(memory_space=pltpu.SMEM)],
        out_specs=pl.BlockSpec(memory_space=pltpu.VMEM),
        name="bias_table",
    )(bucket, rel_bias)


IN_CHUNK = 512


def _inproj_kernel(x_ref, g_ref, w_ref, u_ref, q_ref, kv_ref, h_scr):
    xf = x_ref[...]
    ms = jnp.mean(xf * xf, axis=-1, keepdims=True)
    h_scr[...] = (xf * lax.rsqrt(ms + EPS) * g_ref[...]).astype(BF16)
    for n0 in range(0, IN_WIDTH, IN_CHUNK):
        acc = jnp.dot(h_scr[...], w_ref[:, n0:n0 + IN_CHUNK], preferred_element_type=F32)
        if n0 < POOL_WIDTH:
            u_ref[:, n0:n0 + IN_CHUNK] = acc
        elif n0 < POOL_WIDTH + ATTN_WIDTH:
            q_ref[:, n0 - POOL_WIDTH:n0 - POOL_WIDTH + IN_CHUNK] = acc.astype(BF16)
        else:
            kv_ref[...] = acc.astype(BF16)


def _inproj(x2d, g, w_bf16, tm):
    T = x2d.shape[0]
    return pl.pallas_call(
        _inproj_kernel,
        grid=(T // tm,),
        in_specs=[pl.BlockSpec((tm, D_MODEL), lambda i: (i, 0)),
                  pl.BlockSpec((1, D_MODEL), lambda i: (0, 0)),
                  pl.BlockSpec((D_MODEL, IN_WIDTH), lambda i: (0, 0))],
        out_specs=[pl.BlockSpec((tm, POOL_WIDTH), lambda i: (i, 0)),
                   pl.BlockSpec((tm, ATTN_WIDTH), lambda i: (i, 0)),
                   pl.BlockSpec((tm, KV_WIDTH), lambda i: (i, 0))],
        out_shape=[jax.ShapeDtypeStruct((T, POOL_WIDTH), F32),
                   jax.ShapeDtypeStruct((T, ATTN_WIDTH), BF16),
                   jax.ShapeDtypeStruct((T, KV_WIDTH), BF16)],
        scratch_shapes=[pltpu.VMEM((tm, D_MODEL), BF16)],
        compiler_params=_params(1),
        name="norm1_inproj",
    )(x2d, g, w_bf16)


def _pool_kernel(up_ref, uc_ref, un_ref, pw_ref, ps_ref, gn_ref, o_ref, ext, mix, *, seq):
    s = pl.program_id(1)
    ns = pl.num_programs(1)
    ts = uc_ref.shape[1]
    h = POOL_HALO
    ext[0:h, :] = jnp.where(s > 0, up_ref[0], 0.0)
    ext[h:h + ts, :] = uc_ref[0]
    ext[h + ts:2 * h + ts, :] = jnp.where(s < ns - 1, un_ref[0], 0.0)
    t = s * ts + lax.broadcasted_iota(jnp.int32, (ts, 1), 0)
    for g, w in enumerate(POOL_WINDOWS):
        hw = w // 2
        c0 = g * POOL_GROUP
        acc = ext[h - hw:h - hw + ts, c0:c0 + POOL_GROUP]
        for d in range(-hw + 1, hw):
            acc = acc + ext[h + d:h + d + ts, c0:c0 + POOL_GROUP]
        cnt = (jnp.minimum(t + hw, seq) - jnp.maximum(t - hw, 0)).astype(F32)
        pooled = acc / cnt - ext[h:h + ts, c0:c0 + POOL_GROUP]
        m = jnp.dot(pooled.astype(BF16), pw_ref[g], preferred_element_type=F32)
        mix[:, c0:c0 + POOL_GROUP] = m * ps_ref[:, c0:c0 + POOL_GROUP]
    mm = mix[...]
    ms = jnp.mean(mm * mm, axis=-1, keepdims=True)
    o_ref[0] = (mm * lax.rsqrt(ms + EPS) * gn_ref[...]).astype(BF16)


def _pool(u3d, pool_w_bf16, pool_scale, gn_pool, ts):
    B, S, _ = u3d.shape
    hb = ts // POOL_HALO
    n_hblk = S // POOL_HALO
    return pl.pallas_call(
        functools.partial(_pool_kernel, seq=S),
        grid=(B, S // ts),
        in_specs=[pl.BlockSpec((1, POOL_HALO, POOL_WIDTH), lambda b, s: (b, jnp.maximum(s * hb - 1, 0), 0)),
                  pl.BlockSpec((1, ts, POOL_WIDTH), lambda b, s: (b, s, 0)),
                  pl.BlockSpec((1, POOL_HALO, POOL_WIDTH),
                               lambda b, s: (b, jnp.minimum((s + 1) * hb, n_hblk - 1), 0)),
                  pl.BlockSpec((len(POOL_WINDOWS), POOL_GROUP, POOL_GROUP), lambda b, s: (0, 0, 0)),
                  pl.BlockSpec((1, POOL_WIDTH), lambda b, s: (0, 0)),
                  pl.BlockSpec((1, POOL_WIDTH), lambda b, s: (0, 0))],
        out_specs=pl.BlockSpec((1, ts, POOL_WIDTH), lambda b, s: (b, s, 0)),
        out_shape=jax.ShapeDtypeStruct((B, S, POOL_WIDTH), BF16),
        scratch_shapes=[pltpu.VMEM((ts + 2 * POOL_HALO, POOL_WIDTH), F32),
                        pltpu.VMEM((ts, POOL_WIDTH), F32)],
        compiler_params=_params(2),
        name="pool_mixer",
    )(u3d, u3d, u3d, pool_w_bf16, pool_scale, gn_pool)


def _attn_kernel(q_ref, kvp_ref, kvc_ref, kvn_ref, bias_ref, sink_ref, gn_ref, o_ref, att):
    n = pl.program_id(1)
    nb = pl.num_programs(1)
    scale = HEAD_DIM ** -0.5
    col = lax.broadcasted_iota(jnp.int32, (1, 3 * BLOCK), 1)
    col_ok = jnp.logical_and(jnp.logical_or(n > 0, col >= BLOCK),
                             jnp.logical_or(n < nb - 1, col < 2 * BLOCK))
    col_mask = jnp.where(col_ok, 0.0, MASKED)
    for k in range(N_KV_HEADS):
        ks = slice(k * HEAD_DIM, (k + 1) * HEAD_DIM)
        vs = slice((N_KV_HEADS + k) * HEAD_DIM, (N_KV_HEADS + k + 1) * HEAD_DIM)
        kk = jnp.concatenate([kvp_ref[0, :, ks], kvc_ref[0, :, ks], kvn_ref[0, :, ks]], axis=0)
        vv = jnp.concatenate([kvp_ref[0, :, vs], kvc_ref[0, :, vs], kvn_ref[0, :, vs]], axis=0)
        h0 = k * Q_PER_KV
        qk = jnp.concatenate([q_ref[0, :, (h0 + g) * HEAD_DIM:(h0 + g + 1) * HEAD_DIM]
                              for g in range(Q_PER_KV)], axis=0)
        logits = lax.dot_general(qk, kk, (((1,), (1,)), ((), ())), preferred_element_type=F32)
        bias = bias_ref[h0:h0 + Q_PER_KV].reshape(Q_PER_KV * BLOCK, 3 * BLOCK)
        logits = logits * scale + bias + col_mask
        head = lax.broadcasted_iota(jnp.int32, (Q_PER_KV * BLOCK, 1), 0) // BLOCK
        sink = jnp.full((Q_PER_KV * BLOCK, 1), sink_ref[h0], F32)
        for g in range(1, Q_PER_KV):
            sink = jnp.where(head == g, sink_ref[h0 + g], sink)
        m = jnp.maximum(jnp.max(logits, axis=-1, keepdims=True), sink)
        p = jnp.exp(logits - m)
        denom = jnp.sum(p, axis=-1, keepdims=True) + jnp.exp(sink - m)
        o = jnp.dot(p.astype(BF16), vv, preferred_element_type=F32) / denom
        for g in range(Q_PER_KV):
            att[:, (h0 + g) * HEAD_DIM:(h0 + g + 1) * HEAD_DIM] = o[g * BLOCK:(g + 1) * BLOCK]
    a = att[...]
    ms = jnp.mean(a * a, axis=-1, keepdims=True)
    o_ref[0] = (a * lax.rsqrt(ms + EPS) * gn_ref[...]).astype(BF16)


def _attention(q3d, kv3d, bias_tbl, sink, gn_attn):
    B, S, _ = q3d.shape
    nb = S // BLOCK
    kv_spec = lambda f: pl.BlockSpec((1, BLOCK, KV_WIDTH), f)
    return pl.pallas_call(
        _attn_kernel,
        grid=(B, nb),
        in_specs=[pl.BlockSpec((1, BLOCK, ATTN_WIDTH), lambda b, n: (b, n, 0)),
                  kv_spec(lambda b, n: (b, jnp.maximum(n - 1, 0), 0)),
                  kv_spec(lambda b, n: (b, n, 0)),
                  kv_spec(lambda b, n: (b, jnp.minimum(n + 1, nb - 1), 0)),
                  pl.BlockSpec((N_HEADS, BLOCK, 3 * BLOCK), lambda b, n: (0, 0, 0)),
                  pl.BlockSpec(memory_space=pltpu.SMEM),
                  pl.BlockSpec((1, ATTN_WIDTH), lambda b, n: (0, 0))],
        out_specs=pl.BlockSpec((1, BLOCK, ATTN_WIDTH), lambda b, n: (b, n, 0)),
        out_shape=jax.ShapeDtypeStruct((B, S, ATTN_WIDTH), BF16),
        scratch_shapes=[pltpu.VMEM((BLOCK, ATTN_WIDTH), F32)],
        compiler_params=_params(2),
        name="window_attn",
    )(q3d, kv3d, kv3d, kv3d, bias_tbl, sink, gn_attn)


OUT_CHUNK = 512


def _outproj_kernel(yp_ref, ya_ref, x_ref, wo_ref, g2_ref, wrt_ref, x1_ref, h2_ref, aff_ref, mix):
    tm = x_ref.shape[0]
    mix[:, 0:POOL_WIDTH] = yp_ref[...]
    mix[:, POOL_WIDTH:] = ya_ref[...]
    ss = jnp.zeros((tm, 1), F32)
    for n0 in range(0, D_MODEL, OUT_CHUNK):
        ns = slice(n0, n0 + OUT_CHUNK)
        x1 = x_ref[:, ns] + jnp.dot(mix[...], wo_ref[:, ns], preferred_element_type=F32)
        x1_ref[:, ns] = x1
        ss = ss + jnp.sum(x1 * x1, axis=-1, keepdims=True)
    inv = lax.rsqrt(ss / D_MODEL + EPS)
    h2 = x1_ref[...] * inv * g2_ref[...]
    h2_ref[...] = h2
    logits_t = lax.dot_general(wrt_ref[...], h2.astype(BF16), (((1,), (1,)), ((), ())),
                               preferred_element_type=F32)
    mx = jnp.max(logits_t, axis=0, keepdims=True)
    ex = jnp.exp(logits_t - mx)
    aff_ref[0] = ex / jnp.sum(ex, axis=0, keepdims=True)


def _outproj(yp2d, ya2d, x2d, wo_bf16, g2, wr_t, tm, B, S):
    T = x2d.shape[0]
    per_b = S // tm
    return pl.pallas_call(
        _outproj_kernel,
        grid=(T // tm,),
        in_specs=[pl.BlockSpec((tm, POOL_WIDTH), lambda i: (i, 0)),
                  pl.BlockSpec((tm, ATTN_WIDTH), lambda i: (i, 0)),
                  pl.BlockSpec((tm, D_MODEL), lambda i: (i, 0)),
                  pl.BlockSpec((D_MODEL, D_MODEL), lambda i: (0, 0)),
                  pl.BlockSpec((1, D_MODEL), lambda i: (0, 0)),
                  pl.BlockSpec((N_EXPERTS, D_MODEL), lambda i: (0, 0))],
        out_specs=[pl.BlockSpec((tm, D_MODEL), lambda i: (i, 0)),
                   pl.BlockSpec((tm, D_MODEL), lambda i: (i, 0)),
                   pl.BlockSpec((1, N_EXPERTS, tm), lambda i: (i // per_b, 0, i % per_b))],
        out_shape=[jax.ShapeDtypeStruct((T, D_MODEL), F32),
                   jax.ShapeDtypeStruct((T, D_MODEL), F32),
                   jax.ShapeDtypeStruct((B, N_EXPERTS, S), F32)],
        scratch_shapes=[pltpu.VMEM((tm, POOL_WIDTH + ATTN_WIDTH), BF16)],
        compiler_params=_params(1),
        name="outproj_norm2_router",
    )(yp2d, ya2d, x2d, wo_bf16, g2, wr_t)


def _route_kernel(aff_ref, rows_ref, gate_ref, posm, bounds_vmem, bounds_smem, bounds_sem, *, cap, slot_blk):
    b = pl.program_id(0)
    S = aff_ref.shape[2]
    v = aff_ref[0]
    bits = pltpu.bitcast(v, jnp.int32)

    def count(mask):
        return jnp.sum(jnp.where(mask, 1.0, 0.0), axis=1, keepdims=True)

    thr = jnp.zeros((N_EXPERTS, 1), jnp.int32)
    for bit in range(30, -1, -1):
        cand = thr | (1 << bit)
        thr = jnp.where(count(bits >= cand) >= cap, cand, thr)
    gt = bits > thr
    eq = bits == thr
    need = cap - count(gt)
    tok = lax.broadcasted_iota(jnp.int32, (N_EXPERTS, S), 1)
    bound = jnp.zeros((N_EXPERTS, 1), jnp.int32)
    for bit in range(S.bit_length() - 1, -1, -1):
        cand = bound | (1 << bit)
        bound = jnp.where(count(jnp.logical_and(eq, tok < cand)) <= need, cand, bound)
    sel = jnp.where(jnp.logical_or(gt, jnp.logical_and(eq, tok < bound)), 1.0, 0.0)

    tri = jnp.where(lax.broadcasted_iota(jnp.int32, (LANES, LANES), 0)
                    < lax.broadcasted_iota(jnp.int32, (LANES, LANES), 1), 1.0, 0.0).astype(BF16)
    lane = lax.broadcasted_iota(jnp.int32, (1, LANES), 1)
    n_chunks = S // LANES
    assert n_chunks <= LANES and 2 * (cap // slot_blk) <= LANES
    off = jnp.zeros((N_EXPERTS, 1), F32)
    slots_before = jnp.full((N_EXPERTS, LANES), float(S), F32)
    slots_through = jnp.full((N_EXPERTS, LANES), float(S), F32)
    for c in range(n_chunks):
        sc = sel[:, c * LANES:(c + 1) * LANES]
        within = jnp.dot(sc.astype(BF16), tri, preferred_element_type=F32)
        posm[:, c * LANES:(c + 1) * LANES] = jnp.where(sc > 0.0, within + off, -1.0)
        slots_before = jnp.where(lane == c, off, slots_before)
        off = off + jnp.sum(sc, axis=1, keepdims=True)
        slots_through = jnp.where(lane == c, off, slots_through)

    bounds = jnp.zeros((N_EXPERTS, LANES), jnp.int32)
    for jb in range(cap // slot_blk):
        first = count(slots_through <= float(jb * slot_blk)).astype(jnp.int32)
        last = count(slots_before < float((jb + 1) * slot_blk)).astype(jnp.int32)
        bounds = jnp.where(lane == 2 * jb, first, bounds)
        bounds = jnp.where(lane == 2 * jb + 1, last, bounds)
    bounds_vmem[...] = bounds
    to_smem = pltpu.make_async_copy(bounds_vmem, bounds_smem, bounds_sem)
    to_smem.start()
    to_smem.wait()

    rows_ref[...] = jnp.zeros(rows_ref.shape, jnp.int32)
    gate_ref[...] = jnp.zeros(gate_ref.shape, F32)
    for e in range(N_EXPERTS):
        for jb in range(cap // slot_blk):
            slot = (lax.broadcasted_iota(jnp.int32, (slot_blk, 1), 0) + jb * slot_blk).astype(F32)

            def body(c, carry):
                acc_row, acc_gate = carry
                o = pl.multiple_of(c * LANES, LANES)
                hit = posm[e:e + 1, pl.ds(o, LANES)] == slot
                rowid = (lane + (c * LANES + b * S)).astype(F32)
                acc_row = jnp.where(hit, rowid, acc_row)
                acc_gate = jnp.where(hit, aff_ref[0, e:e + 1, pl.ds(o, LANES)], acc_gate)
                return acc_row, acc_gate

            zero = jnp.zeros((slot_blk, LANES), F32)
            acc_row, acc_gate = lax.fori_loop(bounds_smem[e, 2 * jb], bounds_smem[e, 2 * jb + 1], body,
                                              (zero, zero))
            js = slice(jb * slot_blk, (jb + 1) * slot_blk)
            rows_ref[0, js, e:e + 1] = jnp.sum(acc_row, axis=1, keepdims=True).astype(jnp.int32)
            gate_ref[0, js, e:e + 1] = jnp.sum(acc_gate, axis=1, keepdims=True)


def _route(aff_t, cap):
    B, _, S = aff_t.shape
    slot_blk = min(LANES, cap)
    return pl.pallas_call(
        functools.partial(_route_kernel, cap=cap, slot_blk=slot_blk),
        grid=(B,),
        in_specs=[pl.BlockSpec((1, N_EXPERTS, S), lambda b: (b, 0, 0))],
        out_specs=[pl.BlockSpec((1, cap, LANES), lambda b: (b, 0, 0)),
                   pl.BlockSpec((1, cap, LANES), lambda b: (b, 0, 0))],
        out_shape=[jax.ShapeDtypeStruct((B, cap, LANES), jnp.int32),
                   jax.ShapeDtypeStruct((B, cap, LANES), F32)],
        scratch_shapes=[pltpu.VMEM((N_EXPERTS, S), F32),
                        pltpu.VMEM((N_EXPERTS, LANES), jnp.int32),
                        pltpu.SMEM((N_EXPERTS, LANES), jnp.int32),
                        pltpu.SemaphoreType.DMA(())],
        compiler_params=_params(1),
        name="expert_choice_route",
    )(aff_t)


FF_TILE = 256
N_FF_STEPS = D_FF // FF_TILE


def _ffn_kernel(rows_ref, rows_nxt_ref, gate_ref, h2_hbm, wg_hbm, wu_hbm, wd_hbm, xin_hbm, out_hbm,
                xs, yacc, gcol, stage_x, stage_b, wgb, wub, wdb, sem_x, sem_b, sem_s, sem_w, *, n_rows, m_chunk):
    del xin_hbm
    e = pl.program_id(0)
    f = pl.program_id(1)
    ne = pl.num_programs(0)
    nf = N_FF_STEPS
    chunk = n_rows // nf
    n_m = n_rows // m_chunk
    per_m = chunk // n_m
    cur = lax.rem(e, 2)

    def x_copy(rows_smem, c, r):
        row = rows_smem[0, 0, c * chunk + r]
        return pltpu.make_async_copy(h2_hbm.at[pl.ds(row, 1)], stage_x.at[pl.ds(r, 1)], sem_x)

    def b_copy(rows_smem, c, r):
        row = rows_smem[0, 0, c * chunk + r]
        return pltpu.make_async_copy(out_hbm.at[pl.ds(row, 1)], stage_b.at[pl.ds(r, 1)], sem_b)

    def s_copy(r):
        row = rows_ref[0, 0, r]
        return pltpu.make_async_copy(yacc.at[pl.ds(r, 1)], out_hbm.at[pl.ds(row, 1)], sem_s)

    def wait_x():
        pltpu.make_async_copy(h2_hbm.at[pl.ds(0, chunk)], stage_x, sem_x).wait()

    def wait_b():
        pltpu.make_async_copy(out_hbm.at[pl.ds(0, chunk)], stage_b, sem_b).wait()

    def wait_s(n):
        pltpu.make_async_copy(yacc.at[pl.ds(0, n)], out_hbm.at[pl.ds(0, n)], sem_s).wait()

    def w_copies(ee, ff, slot):
        fo = pl.multiple_of(ff * FF_TILE, FF_TILE)
        return (pltpu.make_async_copy(wg_hbm.at[ee, :, pl.ds(fo, FF_TILE)], wgb.at[slot], sem_w.at[slot]),
                pltpu.make_async_copy(wu_hbm.at[ee, :, pl.ds(fo, FF_TILE)], wub.at[slot], sem_w.at[slot]),
                pltpu.make_async_copy(wd_hbm.at[ee, pl.ds(fo, FF_TILE), :], wdb.at[slot], sem_w.at[slot]))

    def w_start(ee, ff, slot):
        for cp in w_copies(ee, ff, slot):
            cp.start(priority=1)

    def loop_start(copy_fn, n):
        def body(r, carry):
            copy_fn(r).start()
            return carry
        lax.fori_loop(0, n, body, 0, unroll=8)

    @pl.when(jnp.logical_and(e == 0, f == 0))
    def _first_expert_rows():
        for c in range(nf - 1):
            loop_start(lambda r, c=c: x_copy(rows_ref, c, r), chunk)
            wait_x()
            xs[0, c * chunk:(c + 1) * chunk, :] = stage_x[...].astype(BF16)
        loop_start(lambda r: x_copy(rows_ref, nf - 1, r), chunk)
        loop_start(lambda r: b_copy(rows_ref, 0, r), chunk)
        w_start(0, 0, 0)

    @pl.when(f == 0)
    def _per_expert():
        lane = lax.broadcasted_iota(jnp.int32, (1, 1, LANES), 2)
        g = jnp.sum(jnp.where(lane == e, gate_ref[...], 0.0), axis=-1, keepdims=True)
        gcol[...] = g.reshape(n_rows, 1)
        yacc[...] = jnp.zeros(yacc.shape, F32)

    def step(last):
        pc = lax.rem(f + (nf - 1), nf)
        tgt = lax.rem(e + jnp.where(f > 0, 1, 0), 2)
        wait_x()
        xs[tgt, pl.ds(pl.multiple_of(pc * chunk, chunk), chunk), :] = stage_x[...].astype(BF16)
        wait_b()
        yacc[pl.ds(pl.multiple_of(f * chunk, chunk), chunk), :] += stage_b[...]

        wslot = lax.rem(f, 2)
        for cp in w_copies(e, f, wslot):
            cp.wait()
        if not last:
            w_start(e, f + 1, 1 - wslot)
        else:
            @pl.when(e < ne - 1)
            def _next_expert_first_tiles():
                w_start(e + 1, 0, 1 - wslot)
        wg = wgb[wslot].astype(BF16)
        wu = wub[wslot].astype(BF16)
        wd = wdb[wslot].astype(BF16)
        def start_gathers(rs):
            for r in rs:
                x_copy(rows_nxt_ref, f, r).start()
                if not last:
                    b_copy(rows_ref, f + 1, r).start()

        for m in range(n_m):
            rs = list(range(m * per_m, (m + 1) * per_m))
            k = -(-per_m // 3)
            ms = slice(m * m_chunk, (m + 1) * m_chunk)
            start_gathers(rs[:k])
            xm = xs[cur, ms, :]
            a = jnp.dot(xm, wg, preferred_element_type=F32)
            start_gathers(rs[k:2 * k])
            u = jnp.dot(xm, wu, preferred_element_type=F32)
            hm = (a * (1.0 / (1.0 + jnp.exp(-a))) * u).astype(BF16)
            start_gathers(rs[2 * k:])
            yacc[ms, :] += gcol[ms, :] * jnp.dot(hm, wd, preferred_element_type=F32)
            if last:
                for r in range(m * m_chunk, (m + 1) * m_chunk):
                    s_copy(r).start()
        if last:
            for m in range(n_m):
                wait_s(m_chunk)

            @pl.when(e < ne - 1)
            def _next_expert_first_residual_chunk():
                loop_start(lambda r: b_copy(rows_nxt_ref, 0, r), chunk)

            @pl.when(e == ne - 1)
            def _drain_unused_prefetch():
                wait_x()

    @pl.when(f < nf - 1)
    def _():
        step(False)

    @pl.when(f == nf - 1)
    def _():
        step(True)


def _expert_ffn(rows, gate_col, h2, w_gate, w_up, w_down, x1):
    T = x1.shape[0]
    n_rows = rows.shape[-1]
    B, cap, _ = gate_col.shape
    chunk = n_rows // N_FF_STEPS
    m_chunk = min(512, n_rows)
    assert chunk % 8 == 0 and chunk % (n_rows // m_chunk) == 0
    rows_spec = lambda f: pl.BlockSpec((1, 1, n_rows), f, memory_space=pltpu.SMEM)
    return pl.pallas_call(
        functools.partial(_ffn_kernel, n_rows=n_rows, m_chunk=m_chunk),
        grid=(N_EXPERTS, N_FF_STEPS),
        in_specs=[rows_spec(lambda e, f: (e, 0, 0)),
                  rows_spec(lambda e, f: (jnp.minimum(e + 1, N_EXPERTS - 1), 0, 0)),
                  pl.BlockSpec((B, cap, LANES), lambda e, f: (0, 0, 0)),
                  pl.BlockSpec(memory_space=pl.ANY),
                  pl.BlockSpec(memory_space=pl.ANY),
                  pl.BlockSpec(memory_space=pl.ANY),
                  pl.BlockSpec(memory_space=pl.ANY),
                  pl.BlockSpec(memory_space=pl.ANY)],
        out_specs=pl.BlockSpec(memory_space=pl.ANY),
        out_shape=jax.ShapeDtypeStruct((T, D_MODEL), F32),
        input_output_aliases={7: 0},
        scratch_shapes=[pltpu.VMEM((2, n_rows, D_MODEL), BF16),
                        pltpu.VMEM((n_rows, D_MODEL), F32),
                        pltpu.VMEM((n_rows, 1), F32),
                        pltpu.VMEM((chunk, D_MODEL), F32),
                        pltpu.VMEM((chunk, D_MODEL), F32),
                        pltpu.VMEM((2, D_MODEL, FF_TILE), F32),
                        pltpu.VMEM((2, D_MODEL, FF_TILE), F32),
                        pltpu.VMEM((2, FF_TILE, D_MODEL), F32),
                        pltpu.SemaphoreType.DMA(()),
                        pltpu.SemaphoreType.DMA(()),
                        pltpu.SemaphoreType.DMA(()),
                        pltpu.SemaphoreType.DMA((2,))],
        compiler_params=_params(2),
        name="expert_ffn",
    )(rows, rows, gate_col, h2, w_gate, w_up, w_down, x1)


def _final_norm_kernel(x_ref, g_ref, o_ref):
    xf = x_ref[...]
    ms = jnp.mean(xf * xf, axis=-1, keepdims=True)
    o_ref[...] = xf * lax.rsqrt(ms + EPS) * g_ref[...]


def _final_norm(x2d, g, tm):
    T = x2d.shape[0]
    return pl.pallas_call(
        _final_norm_kernel,
        grid=(T // tm,),
        in_specs=[pl.BlockSpec((tm, D_MODEL), lambda i: (i, 0)),
                  pl.BlockSpec((1, D_MODEL), lambda i: (0, 0))],
        out_specs=pl.BlockSpec((tm, D_MODEL), lambda i: (i, 0)),
        out_shape=jax.ShapeDtypeStruct((T, D_MODEL), F32),
        compiler_params=_params(1),
        name="final_norm",
    )(x2d, g)


def kernel(x, norm1_g, w_in, pool_w, pool_scale, rel_bias, sink, gn_pool, gn_attn, w_out, norm2_g,
           w_router, w_gate, w_up, w_down, final_g):
    B, S, D = x.shape
    assert D == D_MODEL and S % BLOCK == 0 and norm1_g.shape[0] == 1
    T = B * S
    cap = CAPACITY_FACTOR * S // N_EXPERTS
    tm = min(512, S)
    row = lambda a: a.reshape(1, -1)

    bias_tbl = _bias_table(rel_bias)
    x2d = x.reshape(T, D)
    u, q, kv = _inproj(x2d, row(norm1_g[0]), w_in[0].astype(BF16), tm)
    y_pool = _pool(u.reshape(B, S, POOL_WIDTH), pool_w[0].astype(BF16), row(pool_scale[0]),
                   row(gn_pool[0]), tm)
    y_attn = _attention(q.reshape(B, S, ATTN_WIDTH), kv.reshape(B, S, KV_WIDTH), bias_tbl, sink[0],
                        row(gn_attn[0]))
    x1, h2, aff_t = _outproj(y_pool.reshape(T, POOL_WIDTH), y_attn.reshape(T, ATTN_WIDTH), x2d,
                             w_out[0].astype(BF16), row(norm2_g[0]), w_router[0].T.astype(BF16), tm, B, S)
    rows_col, gate_col = _route(aff_t, cap)
    rows = rows_col[:, :, :N_EXPERTS].transpose(2, 0, 1).reshape(N_EXPERTS, 1, B * cap)
    x2 = _expert_ffn(rows, gate_col, h2, w_gate[0], w_up[0], w_down[0], x1)
    return _final_norm(x2, row(final_g), tm).reshape(B, S, D)
```

```python
import functools
import math

import numpy as np
import jax
import jax.numpy as jnp
from jax import lax
from jax.experimental import pallas as pl
from jax.experimental.pallas import tpu as pltpu

D_MODEL = 2048
POOL_WIDTH = 1024
POOL_WINDOWS = (2, 4, 8, 16)
POOL_GROUP = POOL_WIDTH // len(POOL_WINDOWS)
POOL_HALO = 8
ATTN_WIDTH = 1024
HEAD_DIM = 128
N_HEADS = ATTN_WIDTH // HEAD_DIM
N_KV_HEADS = 2
Q_PER_KV = N_HEADS // N_KV_HEADS
KV_WIDTH = 2 * N_KV_HEADS * HEAD_DIM
WINDOW = 128
BLOCK = 128
N_BUCKETS = 32
MAX_DISTANCE = 128
N_EXPERTS = 16
CAPACITY_FACTOR = 2
D_FF = 2 * D_MODEL
IN_WIDTH = POOL_WIDTH + ATTN_WIDTH + KV_WIDTH
EPS = 1e-6
MASKED = -1e30
LANES = 128
VMEM_LIMIT_BYTES = 56 * 1024 * 1024

F32 = jnp.float32
BF16 = jnp.bfloat16


def _params(n_axes):
    return pltpu.CompilerParams(dimension_semantics=("arbitrary",) * n_axes,
                                vmem_limit_bytes=VMEM_LIMIT_BYTES)


def _t5_bucket_table():
    half = N_BUCKETS // 2
    max_exact = half // 2
    qi = np.arange(BLOCK)[:, None]
    kj = np.arange(3 * BLOCK)[None, :] - BLOCK
    rel = kj - qi
    n = np.abs(rel)
    nf = np.maximum(n, 1).astype(np.float64)
    large = max_exact + np.floor(np.log(nf / max_exact) / math.log(MAX_DISTANCE / max_exact)
                                 * (half - max_exact) + 1e-9).astype(np.int64)
    large = np.minimum(large, half - 1)
    bucket = np.where(rel > 0, half, 0) + np.where(n < max_exact, n, large)
    return np.where(n <= WINDOW, bucket, -1).astype(np.int32)


def _bias_kernel(bucket_ref, relb_ref, o_ref):
    bk = bucket_ref[...]
    for h in range(N_HEADS):
        acc = jnp.full(bk.shape, MASKED, F32)
        for b in range(N_BUCKETS):
            acc = jnp.where(bk == b, relb_ref[b, h], acc)
        o_ref[h] = acc


def _bias_table(rel_bias):
    bucket = jnp.asarray(_t5_bucket_table())
    return pl.pallas_call(
        _bias_kernel,
        out_shape=jax.ShapeDtypeStruct((N_HEADS, BLOCK, 3 * BLOCK), F32),
        in_specs=[pl.BlockSpec(memory_space=pltpu.VMEM), pl.BlockSpec(memory_space=pltpu.SMEM)],
        out_specs=pl.BlockSpec(memory_space=pltpu.VMEM),
        name="bias_table",
    )(bucket, rel_bias)


IN_CHUNK = 512


def _inproj_kernel(x_ref, g_ref, w_ref, u_ref, q_ref, kv_ref, h_scr):
    xf = x_ref[...]
    ms = jnp.mean(xf * xf, axis=-1, keepdims=True)
    h_scr[...] = (xf * lax.rsqrt(ms + EPS) * g_ref[...]).astype(BF16)
    for n0 in range(0, IN_WIDTH, IN_CHUNK):
        acc = jnp.dot(h_scr[...], w_ref[:, n0:n0 + IN_CHUNK], preferred_element_type=F32)
        if n0 < POOL_WIDTH:
            u_ref[:, n0:n0 + IN_CHUNK] = acc
        elif n0 < POOL_WIDTH + ATTN_WIDTH:
            q_ref[:, n0 - POOL_WIDTH:n0 - POOL_WIDTH + IN_CHUNK] = acc.astype(BF16)
        else:
            kv_ref[...] = acc.astype(BF16)


def _inproj(x2d, g, w_bf16, tm):
    T = x2d.shape[0]
    return pl.pallas_call(
        _inproj_kernel,
        grid=(T // tm,),
        in_specs=[pl.BlockSpec((tm, D_MODEL), lambda i: (i, 0)),
                  pl.BlockSpec((1, D_MODEL), lambda i: (0, 0)),
                  pl.BlockSpec((D_MODEL, IN_WIDTH), lambda i: (0, 0))],
        out_specs=[pl.BlockSpec((tm, POOL_WIDTH), lambda i: (i, 0)),
                   pl.BlockSpec((tm, ATTN_WIDTH), lambda i: (i, 0)),
                   pl.BlockSpec((tm, KV_WIDTH), lambda i: (i, 0))],
        out_shape=[jax.ShapeDtypeStruct((T, POOL_WIDTH), F32),
                   jax.ShapeDtypeStruct((T, ATTN_WIDTH), BF16),
                   jax.ShapeDtypeStruct((T, KV_WIDTH), BF16)],
        scratch_shapes=[pltpu.VMEM((tm, D_MODEL), BF16)],
        compiler_params=_params(1),
        name="norm1_inproj",
    )(x2d, g, w_bf16)


def _pool_kernel(up_ref, uc_ref, un_ref, pw_ref, ps_ref, gn_ref, o_ref, ext, mix, *, seq):
    s = pl.program_id(1)
    ns = pl.num_programs(1)
    ts = uc_ref.shape[1]
    h = POOL_HALO
    ext[0:h, :] = jnp.where(s > 0, up_ref[0], 0.0)
    ext[h:h + ts, :] = uc_ref[0]
    ext[h + ts:2 * h + ts, :] = jnp.where(s < ns - 1, un_ref[0], 0.0)
    t = s * ts + lax.broadcasted_iota(jnp.int32, (ts, 1), 0)
    for g, w in enumerate(POOL_WINDOWS):
        hw = w // 2
        c0 = g * POOL_GROUP
        acc = ext[h - hw:h - hw + ts, c0:c0 + POOL_GROUP]
        for d in range(-hw + 1, hw):
            acc = acc + ext[h + d:h + d + ts, c0:c0 + POOL_GROUP]
        cnt = (jnp.minimum(t + hw, seq) - jnp.maximum(t - hw, 0)).astype(F32)
        pooled = acc / cnt - ext[h:h + ts, c0:c0 + POOL_GROUP]
        m = jnp.dot(pooled.astype(BF16), pw_ref[g], preferred_element_type=F32)
        mix[:, c0:c0 + POOL_GROUP] = m * ps_ref[:, c0:c0 + POOL_GROUP]
    mm = mix[...]
    ms = jnp.mean(mm * mm, axis=-1, keepdims=True)
    o_ref[0] = (mm * lax.rsqrt(ms + EPS) * gn_ref[...]).astype(BF16)


def _pool(u3d, pool_w_bf16, pool_scale, gn_pool, ts):
    B, S, _ = u3d.shape
    hb = ts // POOL_HALO
    n_hblk = S // POOL_HALO
    return pl.pallas_call(
        functools.partial(_pool_kernel, seq=S),
        grid=(B, S // ts),
        in_specs=[pl.BlockSpec((1, POOL_HALO, POOL_WIDTH), lambda b, s: (b, jnp.maximum(s * hb - 1, 0), 0)),
                  pl.BlockSpec((1, ts, POOL_WIDTH), lambda b, s: (b, s, 0)),
                  pl.BlockSpec((1, POOL_HALO, POOL_WIDTH),
                               lambda b, s: (b, jnp.minimum((s + 1) * hb, n_hblk - 1), 0)),
                  pl.BlockSpec((len(POOL_WINDOWS), POOL_GROUP, POOL_GROUP), lambda b, s: (0, 0, 0)),
                  pl.BlockSpec((1, POOL_WIDTH), lambda b, s: (0, 0)),
                  pl.BlockSpec((1, POOL_WIDTH), lambda b, s: (0, 0))],
        out_specs=pl.BlockSpec((1, ts, POOL_WIDTH), lambda b, s: (b, s, 0)),
        out_shape=jax.ShapeDtypeStruct((B, S, POOL_WIDTH), BF16),
        scratch_shapes=[pltpu.VMEM((ts + 2 * POOL_HALO, POOL_WIDTH), F32),
                        pltpu.VMEM((ts, POOL_WIDTH), F32)],
        compiler_params=_params(2),
        name="pool_mixer",
    )(u3d, u3d, u3d, pool_w_bf16, pool_scale, gn_pool)


ATTN_BLOCKS_PER_STEP = 2


def _attn_kernel(q_ref, kvp_ref, kvc_ref, kvn_ref, bias_ref, sink_ref, gn_ref, o_ref, att):
    nsub = ATTN_BLOCKS_PER_STEP
    n = pl.program_id(1)
    nb = pl.num_programs(1) * nsub
    scale = HEAD_DIM ** -0.5
    col = lax.broadcasted_iota(jnp.int32, (1, 3 * BLOCK), 1)
    head = lax.broadcasted_iota(jnp.int32, (Q_PER_KV * BLOCK, 1), 0) // BLOCK

    def kv_block(j, cols):
        if j == 0:
            return kvp_ref[0, :, cols]
        if j == nsub + 1:
            return kvn_ref[0, :, cols]
        return kvc_ref[0, (j - 1) * BLOCK:j * BLOCK, cols]

    for sub in range(nsub):
        blk = n * nsub + sub
        rows = slice(sub * BLOCK, (sub + 1) * BLOCK)
        col_ok = jnp.logical_and(jnp.logical_or(blk > 0, col >= BLOCK),
                                 jnp.logical_or(blk < nb - 1, col < 2 * BLOCK))
        col_mask = jnp.where(col_ok, 0.0, MASKED)
        for k in range(N_KV_HEADS):
            ks = slice(k * HEAD_DIM, (k + 1) * HEAD_DIM)
            vs = slice((N_KV_HEADS + k) * HEAD_DIM, (N_KV_HEADS + k + 1) * HEAD_DIM)
            kk = jnp.concatenate([kv_block(sub + j, ks) for j in range(3)], axis=0)
            vv = jnp.concatenate([kv_block(sub + j, vs) for j in range(3)], axis=0)
            h0 = k * Q_PER_KV
            qk = jnp.concatenate([q_ref[0, rows, (h0 + g) * HEAD_DIM:(h0 + g + 1) * HEAD_DIM]
                                  for g in range(Q_PER_KV)], axis=0)
            logits = lax.dot_general(qk, kk, (((1,), (1,)), ((), ())), preferred_element_type=F32)
            bias = bias_ref[h0:h0 + Q_PER_KV].reshape(Q_PER_KV * BLOCK, 3 * BLOCK)
            logits = logits * scale + bias + col_mask
            sink = jnp.full((Q_PER_KV * BLOCK, 1), sink_ref[h0], F32)
            for g in range(1, Q_PER_KV):
                sink = jnp.where(head == g, sink_ref[h0 + g], sink)
            m = jnp.maximum(jnp.max(logits, axis=-1, keepdims=True), sink)
            p = jnp.exp(logits - m)
            denom = jnp.sum(p, axis=-1, keepdims=True) + jnp.exp(sink - m)
            o = jnp.dot(p.astype(BF16), vv, preferred_element_type=F32) / denom
            for g in range(Q_PER_KV):
                att[rows, (h0 + g) * HEAD_DIM:(h0 + g + 1) * HEAD_DIM] = o[g * BLOCK:(g + 1) * BLOCK]
    a = att[...]
    ms = jnp.mean(a * a, axis=-1, keepdims=True)
    o_ref[0] = (a * lax.rsqrt(ms + EPS) * gn_ref[...]).astype(BF16)


def _attention(q3d, kv3d, bias_tbl, sink, gn_attn):
    B, S, _ = q3d.shape
    nb = S // BLOCK
    nsub = ATTN_BLOCKS_PER_STEP
    assert nb % nsub == 0
    rows = nsub * BLOCK
    kv_spec = lambda f: pl.BlockSpec((1, BLOCK, KV_WIDTH), f)
    return pl.pallas_call(
        _attn_kernel,
        grid=(B, nb // nsub),
        in_specs=[pl.BlockSpec((1, rows, ATTN_WIDTH), lambda b, n: (b, n, 0)),
                  kv_spec(lambda b, n: (b, jnp.maximum(n * nsub - 1, 0), 0)),
                  pl.BlockSpec((1, rows, KV_WIDTH), lambda b, n: (b, n, 0)),
                  kv_spec(lambda b, n: (b, jnp.minimum((n + 1) * nsub, nb - 1), 0)),
                  pl.BlockSpec((N_HEADS, BLOCK, 3 * BLOCK), lambda b, n: (0, 0, 0)),
                  pl.BlockSpec(memory_space=pltpu.SMEM),
                  pl.BlockSpec((1, ATTN_WIDTH), lambda b, n: (0, 0))],
        out_specs=pl.BlockSpec((1, rows, ATTN_WIDTH), lambda b, n: (b, n, 0)),
        out_shape=jax.ShapeDtypeStruct((B, S, ATTN_WIDTH), BF16),
        scratch_shapes=[pltpu.VMEM((rows, ATTN_WIDTH), F32)],
        compiler_params=_params(2),
        name="window_attn",
    )(q3d, kv3d, kv3d, kv3d, bias_tbl, sink, gn_attn)


OUT_CHUNK = 512


def _outproj_kernel(yp_ref, ya_ref, x_ref, wo_ref, g2_ref, wrt_ref, x1_ref, h2_ref, aff_ref, mix):
    tm = x_ref.shape[0]
    mix[:, 0:POOL_WIDTH] = yp_ref[...]
    mix[:, POOL_WIDTH:] = ya_ref[...]
    ss = jnp.zeros((tm, 1), F32)
    for n0 in range(0, D_MODEL, OUT_CHUNK):
        ns = slice(n0, n0 + OUT_CHUNK)
        x1 = x_ref[:, ns] + jnp.dot(mix[...], wo_ref[:, ns], preferred_element_type=F32)
        x1_ref[:, ns] = x1
        ss = ss + jnp.sum(x1 * x1, axis=-1, keepdims=True)
    inv = lax.rsqrt(ss / D_MODEL + EPS)
    h2 = x1_ref[...] * inv * g2_ref[...]
    h2_ref[...] = h2
    logits_t = lax.dot_general(wrt_ref[...], h2.astype(BF16), (((1,), (1,)), ((), ())),
                               preferred_element_type=F32)
    mx = jnp.max(logits_t, axis=0, keepdims=True)
    ex = jnp.exp(logits_t - mx)
    aff_ref[0] = ex / jnp.sum(ex, axis=0, keepdims=True)


def _outproj(yp2d, ya2d, x2d, wo_bf16, g2, wr_t, tm, B, S):
    T = x2d.shape[0]
    per_b = S // tm
    return pl.pallas_call(
        _outproj_kernel,
        grid=(T // tm,),
        in_specs=[pl.BlockSpec((tm, POOL_WIDTH), lambda i: (i, 0)),
                  pl.BlockSpec((tm, ATTN_WIDTH), lambda i: (i, 0)),
                  pl.BlockSpec((tm, D_MODEL), lambda i: (i, 0)),
                  pl.BlockSpec((D_MODEL, D_MODEL), lambda i: (0, 0)),
                  pl.BlockSpec((1, D_MODEL), lambda i: (0, 0)),
                  pl.BlockSpec((N_EXPERTS, D_MODEL), lambda i: (0, 0))],
        out_specs=[pl.BlockSpec((tm, D_MODEL), lambda i: (i, 0)),
                   pl.BlockSpec((tm, D_MODEL), lambda i: (i, 0)),
                   pl.BlockSpec((1, N_EXPERTS, tm), lambda i: (i // per_b, 0, i % per_b))],
        out_shape=[jax.ShapeDtypeStruct((T, D_MODEL), F32),
                   jax.ShapeDtypeStruct((T, D_MODEL), F32),
                   jax.ShapeDtypeStruct((B, N_EXPERTS, S), F32)],
        scratch_shapes=[pltpu.VMEM((tm, POOL_WIDTH + ATTN_WIDTH), BF16)],
        compiler_params=_params(1),
        name="outproj_norm2_router",
    )(yp2d, ya2d, x2d, wo_bf16, g2, wr_t)


def _route_kernel(aff_ref, rows_ref, gate_ref, posm, bounds_vmem, bounds_smem, bounds_sem, *, cap, slot_blk):
    b = pl.program_id(0)
    S = aff_ref.shape[2]
    v = aff_ref[0]
    bits = pltpu.bitcast(v, jnp.int32)

    def count(mask):
        return jnp.sum(jnp.where(mask, 1.0, 0.0), axis=1, keepdims=True)

    thr = jnp.zeros((N_EXPERTS, 1), jnp.int32)
    for bit in range(30, -1, -1):
        cand = thr | (1 << bit)
        thr = jnp.where(count(bits >= cand) >= cap, cand, thr)
    gt = bits > thr
    eq = bits == thr
    need = cap - count(gt)
    tok = lax.broadcasted_iota(jnp.int32, (N_EXPERTS, S), 1)
    bound = jnp.zeros((N_EXPERTS, 1), jnp.int32)
    for bit in range(S.bit_length() - 1, -1, -1):
        cand = bound | (1 << bit)
        bound = jnp.where(count(jnp.logical_and(eq, tok < cand)) <= need, cand, bound)
    sel = jnp.where(jnp.logical_or(gt, jnp.logical_and(eq, tok < bound)), 1.0, 0.0)

    tri = jnp.where(lax.broadcasted_iota(jnp.int32, (LANES, LANES), 0)
                    < lax.broadcasted_iota(jnp.int32, (LANES, LANES), 1), 1.0, 0.0).astype(BF16)
    lane = lax.broadcasted_iota(jnp.int32, (1, LANES), 1)
    n_chunks = S // LANES
    assert n_chunks <= LANES and 2 * (cap // slot_blk) <= LANES
    off = jnp.zeros((N_EXPERTS, 1), F32)
    slots_before = jnp.full((N_EXPERTS, LANES), float(S), F32)
    slots_through = jnp.full((N_EXPERTS, LANES), float(S), F32)
    for c in range(n_chunks):
        sc = sel[:, c * LANES:(c + 1) * LANES]
        within = jnp.dot(sc.astype(BF16), tri, preferred_element_type=F32)
        posm[:, c * LANES:(c + 1) * LANES] = jnp.where(sc > 0.0, within + off, -1.0)
        slots_before = jnp.where(lane == c, off, slots_before)
        off = off + jnp.sum(sc, axis=1, keepdims=True)
        slots_through = jnp.where(lane == c, off, slots_through)

    bounds = jnp.zeros((N_EXPERTS, LANES), jnp.int32)
    for jb in range(cap // slot_blk):
        first = count(slots_through <= float(jb * slot_blk)).astype(jnp.int32)
        last = count(slots_before < float((jb + 1) * slot_blk)).astype(jnp.int32)
        bounds = jnp.where(lane == 2 * jb, first, bounds)
        bounds = jnp.where(lane == 2 * jb + 1, last, bounds)
    bounds_vmem[...] = bounds
    to_smem = pltpu.make_async_copy(bounds_vmem, bounds_smem, bounds_sem)
    to_smem.start()
    to_smem.wait()

    rows_ref[...] = jnp.zeros(rows_ref.shape, jnp.int32)
    gate_ref[...] = jnp.zeros(gate_ref.shape, F32)
    for e in range(N_EXPERTS):
        for jb in range(cap // slot_blk):
            slot = (lax.broadcasted_iota(jnp.int32, (slot_blk, 1), 0) + jb * slot_blk).astype(F32)

            def body(c, carry):
                acc_row, acc_gate = carry
                o = pl.multiple_of(c * LANES, LANES)
                hit = posm[e:e + 1, pl.ds(o, LANES)] == slot
                rowid = (lane + (c * LANES + b * S)).astype(F32)
                acc_row = jnp.where(hit, rowid, acc_row)
                acc_gate = jnp.where(hit, aff_ref[0, e:e + 1, pl.ds(o, LANES)], acc_gate)
                return acc_row, acc_gate

            zero = jnp.zeros((slot_blk, LANES), F32)
            acc_row, acc_gate = lax.fori_loop(bounds_smem[e, 2 * jb], bounds_smem[e, 2 * jb + 1], body,
                                              (zero, zero))
            js = slice(jb * slot_blk, (jb + 1) * slot_blk)
            rows_ref[0, js, e:e + 1] = jnp.sum(acc_row, axis=1, keepdims=True).astype(jnp.int32)
            gate_ref[0, js, e:e + 1] = jnp.sum(acc_gate, axis=1, keepdims=True)


def _route(aff_t, cap):
    B, _, S = aff_t.shape
    slot_blk = min(LANES, cap)
    return pl.pallas_call(
        functools.partial(_route_kernel, cap=cap, slot_blk=slot_blk),
        grid=(B,),
        in_specs=[pl.BlockSpec((1, N_EXPERTS, S), lambda b: (b, 0, 0))],
        out_specs=[pl.BlockSpec((1, cap, LANES), lambda b: (b, 0, 0)),
                   pl.BlockSpec((1, cap, LANES), lambda b: (b, 0, 0))],
        out_shape=[jax.ShapeDtypeStruct((B, cap, LANES), jnp.int32),
                   jax.ShapeDtypeStruct((B, cap, LANES), F32)],
        scratch_shapes=[pltpu.VMEM((N_EXPERTS, S), F32),
                        pltpu.VMEM((N_EXPERTS, LANES), jnp.int32),
                        pltpu.SMEM((N_EXPERTS, LANES), jnp.int32),
                        pltpu.SemaphoreType.DMA(())],
        compiler_params=_params(1),
        name="expert_choice_route",
    )(aff_t)


FF_TILE = 256
N_FF_STEPS = D_FF // FF_TILE


def _ffn_kernel(rows_ref, rows_nxt_ref, gate_ref, h2_hbm, wg_hbm, wu_hbm, wd_hbm, xin_hbm, out_hbm,
                xs, yacc, gcol, stage_x, stage_b, wgb, wub, wdb, sem_x, sem_b, sem_s, sem_w, *, n_rows, m_chunk):
    del xin_hbm
    e = pl.program_id(0)
    f = pl.program_id(1)
    ne = pl.num_programs(0)
    nf = N_FF_STEPS
    chunk = n_rows // nf
    n_m = n_rows // m_chunk
    per_m = chunk // n_m
    cur = lax.rem(e, 2)

    def x_copy(rows_smem, c, r):
        row = rows_smem[0, 0, c * chunk + r]
        return pltpu.make_async_copy(h2_hbm.at[pl.ds(row, 1)], stage_x.at[pl.ds(r, 1)], sem_x)

    def b_copy(rows_smem, c, r):
        row = rows_smem[0, 0, c * chunk + r]
        return pltpu.make_async_copy(out_hbm.at[pl.ds(row, 1)], stage_b.at[pl.ds(r, 1)], sem_b)

    def s_copy(r):
        row = rows_ref[0, 0, r]
        return pltpu.make_async_copy(yacc.at[pl.ds(r, 1)], out_hbm.at[pl.ds(row, 1)], sem_s)

    def wait_x():
        pltpu.make_async_copy(h2_hbm.at[pl.ds(0, chunk)], stage_x, sem_x).wait()

    def wait_b():
        pltpu.make_async_copy(out_hbm.at[pl.ds(0, chunk)], stage_b, sem_b).wait()

    def wait_s(n):
        pltpu.make_async_copy(yacc.at[pl.ds(0, n)], out_hbm.at[pl.ds(0, n)], sem_s).wait()

    def w_copies(ee, ff, slot):
        fo = pl.multiple_of(ff * FF_TILE, FF_TILE)
        return (pltpu.make_async_copy(wg_hbm.at[ee, :, pl.ds(fo, FF_TILE)], wgb.at[slot], sem_w.at[slot]),
                pltpu.make_async_copy(wu_hbm.at[ee, :, pl.ds(fo, FF_TILE)], wub.at[slot], sem_w.at[slot]),
                pltpu.make_async_copy(wd_hbm.at[ee, pl.ds(fo, FF_TILE), :], wdb.at[slot], sem_w.at[slot]))

    def w_start(ee, ff, slot):
        for cp in w_copies(ee, ff, slot):
            cp.start(priority=1)

    def loop_start(copy_fn, n):
        def body(r, carry):
            copy_fn(r).start()
            return carry
        lax.fori_loop(0, n, body, 0, unroll=8)

    @pl.when(jnp.logical_and(e == 0, f == 0))
    def _first_expert_rows():
        for c in range(nf - 1):
            loop_start(lambda r, c=c: x_copy(rows_ref, c, r), chunk)
            wait_x()
            xs[0, c * chunk:(c + 1) * chunk, :] = stage_x[...].astype(BF16)
        loop_start(lambda r: x_copy(rows_ref, nf - 1, r), chunk)
        loop_start(lambda r: b_copy(rows_ref, 0, r), chunk)
        w_start(0, 0, 0)

    @pl.when(f == 0)
    def _per_expert():
        lane = lax.broadcasted_iota(jnp.int32, (1, 1, LANES), 2)
        g = jnp.sum(jnp.where(lane == e, gate_ref[...], 0.0), axis=-1, keepdims=True)
        gcol[...] = g.reshape(n_rows, 1)

    def split3(rs):
        k = -(-len(rs) // 3)
        return rs[:k], rs[k:2 * k], rs[2 * k:]

    def step(first, last):
        pc = lax.rem(f + (nf - 1), nf)
        tgt = lax.rem(e + jnp.where(f > 0, 1, 0), 2)
        wait_x()
        xs[tgt, pl.ds(pl.multiple_of(pc * chunk, chunk), chunk), :] = stage_x[...].astype(BF16)
        if not first:
            wait_b()
            yacc[pl.ds(pl.multiple_of(f * chunk, chunk), chunk), :] += stage_b[...]

        wslot = lax.rem(f, 2)
        for cp in w_copies(e, f, wslot):
            cp.wait()
        if not last:
            w_start(e, f + 1, 1 - wslot)
        else:
            @pl.when(e < ne - 1)
            def _next_expert_first_tiles():
                w_start(e + 1, 0, 1 - wslot)
        wg = wgb[wslot].astype(BF16)
        wu = wub[wslot].astype(BF16)
        wd = wdb[wslot].astype(BF16)

        x_by_m = [list(range(m * per_m, (m + 1) * per_m)) for m in range(n_m)]
        all_b = list(range(chunk))
        b_tail = []
        if last:
            b_by_m = [[] for _ in range(n_m)]
        elif not first:
            b_by_m = x_by_m
        elif n_m > 1:
            per = -(-chunk // (n_m - 1))
            b_by_m = [[]] + [all_b[i * per:(i + 1) * per] for i in range(n_m - 1)]
        else:
            b_by_m, b_tail = [[]], all_b

        def start_gathers(xr, br):
            for r in xr:
                x_copy(rows_nxt_ref, f, r).start()
            for r in br:
                b_copy(rows_ref, f + 1, r).start(priority=1)

        for m in range(n_m):
            xr, br = split3(x_by_m[m]), split3(b_by_m[m])
            ms = slice(m * m_chunk, (m + 1) * m_chunk)
            start_gathers(xr[0], br[0])
            xm = xs[cur, ms, :]
            a = jnp.dot(xm, wg, preferred_element_type=F32)
            start_gathers(xr[1], br[1])
            u = jnp.dot(xm, wu, preferred_element_type=F32)
            hm = (a * (1.0 / (1.0 + jnp.exp(-a))) * u).astype(BF16)
            start_gathers(xr[2], br[2])
            y = gcol[ms, :] * jnp.dot(hm, wd, preferred_element_type=F32)
            if first:
                yacc[ms, :] = y
                if m == 0:
                    wait_b()
                    yacc[0:chunk, :] += stage_b[...]
                    start_gathers([], b_tail)
            else:
                yacc[ms, :] += y
            if last:
                for r in range(m * m_chunk, (m + 1) * m_chunk):
                    s_copy(r).start()
        if last:
            for m in range(n_m):
                wait_s(m_chunk)

            @pl.when(e < ne - 1)
            def _next_expert_first_residual_chunk():
                loop_start(lambda r: b_copy(rows_nxt_ref, 0, r), chunk)

            @pl.when(e == ne - 1)
            def _drain_unused_prefetch():
                wait_x()

    @pl.when(f == 0)
    def _():
        step(True, False)

    @pl.when(jnp.logical_and(f > 0, f < nf - 1))
    def _():
        step(False, False)

    @pl.when(f == nf - 1)
    def _():
        step(False, True)


def _expert_ffn(rows, gate_col, h2, w_gate, w_up, w_down, x1):
    T = x1.shape[0]
    n_rows = rows.shape[-1]
    B, cap, _ = gate_col.shape
    chunk = n_rows // N_FF_STEPS
    m_chunk = min(512, n_rows)
    assert chunk % 8 == 0 and chunk % (n_rows // m_chunk) == 0
    rows_spec = lambda f: pl.BlockSpec((1, 1, n_rows), f, memory_space=pltpu.SMEM)
    return pl.pallas_call(
        functools.partial(_ffn_kernel, n_rows=n_rows, m_chunk=m_chunk),
        grid=(N_EXPERTS, N_FF_STEPS),
        in_specs=[rows_spec(lambda e, f: (e, 0, 0)),
                  rows_spec(lambda e, f: (jnp.minimum(e + 1, N_EXPERTS - 1), 0, 0)),
                  pl.BlockSpec((B, cap, LANES), lambda e, f: (0, 0, 0)),
                  pl.BlockSpec(memory_space=pl.ANY),
                  pl.BlockSpec(memory_space=pl.ANY),
                  pl.BlockSpec(memory_space=pl.ANY),
                  pl.BlockSpec(memory_space=pl.ANY),
                  pl.BlockSpec(memory_space=pl.ANY)],
        out_specs=pl.BlockSpec(memory_space=pl.ANY),
        out_shape=jax.ShapeDtypeStruct((T, D_MODEL), F32),
        input_output_aliases={7: 0},
        scratch_shapes=[pltpu.VMEM((2, n_rows, D_MODEL), BF16),
                        pltpu.VMEM((n_rows, D_MODEL), F32),
                        pltpu.VMEM((n_rows, 1), F32),
                        pltpu.VMEM((chunk, D_MODEL), F32),
                        pltpu.VMEM((chunk, D_MODEL), F32),
                        pltpu.VMEM((2, D_MODEL, FF_TILE), F32),
                        pltpu.VMEM((2, D_MODEL, FF_TILE), F32),
                        pltpu.VMEM((2, FF_TILE, D_MODEL), F32),
                        pltpu.SemaphoreType.DMA(()),
                        pltpu.SemaphoreType.DMA(()),
                        pltpu.SemaphoreType.DMA(()),
                        pltpu.SemaphoreType.DMA((2,))],
        compiler_params=_params(2),
        name="expert_ffn",
    )(rows, rows, gate_col, h2, w_gate, w_up, w_down, x1)


def _final_norm_kernel(x_ref, g_ref, o_ref):
    xf = x_ref[...]
    ms = jnp.mean(xf * xf, axis=-1, keepdims=True)
    o_ref[...] = xf * lax.rsqrt(ms + EPS) * g_ref[...]


def _final_norm(x2d, g, tm):
    T = x2d.shape[0]
    return pl.pallas_call(
        _final_norm_kernel,
        grid=(T // tm,),
        in_specs=[pl.BlockSpec((tm, D_MODEL), lambda i: (i, 0)),
                  pl.BlockSpec((1, D_MODEL), lambda i: (0, 0))],
        out_specs=pl.BlockSpec((tm, D_MODEL), lambda i: (i, 0)),
        out_shape=jax.ShapeDtypeStruct((T, D_MODEL), F32),
        compiler_params=_params(1),
        name="final_norm",
    )(x2d, g)


def kernel(x, norm1_g, w_in, pool_w, pool_scale, rel_bias, sink, gn_pool, gn_attn, w_out, norm2_g,
           w_router, w_gate, w_up, w_down, final_g):
    B, S, D = x.shape
    assert D == D_MODEL and S % BLOCK == 0 and norm1_g.shape[0] == 1
    T = B * S
    cap = CAPACITY_FACTOR * S // N_EXPERTS
    tm = min(512, S)
    row = lambda a: a.reshape(1, -1)

    bias_tbl = _bias_table(rel_bias)
    x2d = x.reshape(T, D)
    u, q, kv = _inproj(x2d, row(norm1_g[0]), w_in[0].astype(BF16), tm)
    y_pool = _pool(u.reshape(B, S, POOL_WIDTH), pool_w[0].astype(BF16), row(pool_scale[0]),
                   row(gn_pool[0]), tm)
    y_attn = _attention(q.reshape(B, S, ATTN_WIDTH), kv.reshape(B, S, KV_WIDTH), bias_tbl, sink[0],
                        row(gn_attn[0]))
    x1, h2, aff_t = _outproj(y_pool.reshape(T, POOL_WIDTH), y_attn.reshape(T, ATTN_WIDTH), x2d,
                             w_out[0].astype(BF16), row(norm2_g[0]), w_router[0].T.astype(BF16), tm, B, S)
    rows_col, gate_col = _route(aff_t, cap)
    rows = rows_col[:, :, :N_EXPERTS].transpose(2, 0, 1).reshape(N_EXPERTS, 1, B * cap)
    x2 = _expert_ffn(rows, gate_col, h2, w_gate[0], w_up[0], w_down[0], x1)
    return _final_norm(x2, row(final_g), tm).reshape(B, S, D)
```

```python
import functools
import math

import numpy as np
import jax
import jax.numpy as jnp
from jax import lax
from jax.experimental import pallas as pl
from jax.experimental.pallas import tpu as pltpu

D_MODEL = 2048
POOL_WIDTH = 1024
POOL_WINDOWS = (2, 4, 8, 16)
POOL_GROUP = POOL_WIDTH // len(POOL_WINDOWS)
POOL_HALO = 8
ATTN_WIDTH = 1024
HEAD_DIM = 128
N_HEADS = ATTN_WIDTH // HEAD_DIM
N_KV_HEADS = 2
Q_PER_KV = N_HEADS // N_KV_HEADS
KV_WIDTH = 2 * N_KV_HEADS * HEAD_DIM
WINDOW = 128
BLOCK = 128
N_BUCKETS = 32
MAX_DISTANCE = 128
N_EXPERTS = 16
CAPACITY_FACTOR = 2
D_FF = 2 * D_MODEL
IN_WIDTH = POOL_WIDTH + ATTN_WIDTH + KV_WIDTH
EPS = 1e-6
MASKED = -1e30
LOG2E = math.log2(math.e)
LANES = 128
VMEM_LIMIT_BYTES = 58 * 1024 * 1024

F32 = jnp.float32
BF16 = jnp.bfloat16


def _params(n_axes):
    return pltpu.CompilerParams(dimension_semantics=("arbitrary",) * n_axes,
                                vmem_limit_bytes=VMEM_LIMIT_BYTES)


def _t5_bucket_table():
    half = N_BUCKETS // 2
    max_exact = half // 2
    qi = np.arange(BLOCK)[:, None]
    kj = np.arange(3 * BLOCK)[None, :] - BLOCK
    rel = kj - qi
    n = np.abs(rel)
    nf = np.maximum(n, 1).astype(np.float64)
    large = max_exact + np.floor(np.log(nf / max_exact) / math.log(MAX_DISTANCE / max_exact)
                                 * (half - max_exact) + 1e-9).astype(np.int64)
    large = np.minimum(large, half - 1)
    bucket = np.where(rel > 0, half, 0) + np.where(n < max_exact, n, large)
    return np.where(n <= WINDOW, bucket, -1).astype(np.int32)


def _bias_kernel(bucket_ref, relb_ref, o_ref):
    bk = bucket_ref[...]
    for h in range(N_HEADS):
        acc = jnp.full(bk.shape, MASKED, F32)
        for b in range(N_BUCKETS):
            acc = jnp.where(bk == b, relb_ref[b, h] * LOG2E, acc)
        o_ref[h] = acc


def _bias_table(rel_bias):
    bucket = jnp.asarray(_t5_bucket_table())
    return pl.pallas_call(
        _bias_kernel,
        out_shape=jax.ShapeDtypeStruct((N_HEADS, BLOCK, 3 * BLOCK), F32),
        in_specs=[pl.BlockSpec(memory_space=pltpu.VMEM), pl.BlockSpec(memory_space=pltpu.SMEM)],
        out_specs=pl.BlockSpec(memory_space=pltpu.VMEM),
        name="bias_table",
    )(bucket, rel_bias)


def _inproj_kernel(x_ref, g_ref, w_ref, u_ref, q_ref, kv_ref, h_scr):
    xf = x_ref[...]
    ms = jnp.mean(xf * xf, axis=-1, keepdims=True)
    h_scr[...] = (xf * lax.rsqrt(ms + EPS) * g_ref[...]).astype(BF16)
    q0, kv0 = POOL_WIDTH, POOL_WIDTH + ATTN_WIDTH
    u_ref[...] = jnp.dot(h_scr[...], w_ref[:, :q0], preferred_element_type=F32)
    q_ref[...] = jnp.dot(h_scr[...], w_ref[:, q0:kv0], preferred_element_type=F32).astype(BF16)
    kv_ref[...] = jnp.dot(h_scr[...], w_ref[:, kv0:], preferred_element_type=F32).astype(BF16)


def _inproj(x2d, g, w_bf16, tm):
    T = x2d.shape[0]
    return pl.pallas_call(
        _inproj_kernel,
        grid=(T // tm,),
        in_specs=[pl.BlockSpec((tm, D_MODEL), lambda i: (i, 0)),
                  pl.BlockSpec((1, D_MODEL), lambda i: (0, 0)),
                  pl.BlockSpec((D_MODEL, IN_WIDTH), lambda i: (0, 0))],
        out_specs=[pl.BlockSpec((tm, POOL_WIDTH), lambda i: (i, 0)),
                   pl.BlockSpec((tm, ATTN_WIDTH), lambda i: (i, 0)),
                   pl.BlockSpec((tm, KV_WIDTH), lambda i: (i, 0))],
        out_shape=[jax.ShapeDtypeStruct((T, POOL_WIDTH), F32),
                   jax.ShapeDtypeStruct((T, ATTN_WIDTH), BF16),
                   jax.ShapeDtypeStruct((T, KV_WIDTH), BF16)],
        scratch_shapes=[pltpu.VMEM((tm, D_MODEL), BF16)],
        compiler_params=_params(1),
        name="norm1_inproj",
    )(x2d, g, w_bf16)


def _pool_kernel(up_ref, uc_ref, un_ref, pw_ref, ps_ref, gn_ref, o_ref, ext, mix, *, seq):
    s = pl.program_id(1)
    ns = pl.num_programs(1)
    ts = uc_ref.shape[1]
    h = POOL_HALO
    ext[0:h, :] = jnp.where(s > 0, up_ref[0], 0.0)
    ext[h:h + ts, :] = uc_ref[0]
    ext[h + ts:2 * h + ts, :] = jnp.where(s < ns - 1, un_ref[0], 0.0)
    t = s * ts + lax.broadcasted_iota(jnp.int32, (ts, 1), 0)
    for g, w in enumerate(POOL_WINDOWS):
        hw = w // 2
        c0 = g * POOL_GROUP
        acc = ext[h - hw:h - hw + ts, c0:c0 + POOL_GROUP]
        for d in range(-hw + 1, hw):
            acc = acc + ext[h + d:h + d + ts, c0:c0 + POOL_GROUP]
        cnt = (jnp.minimum(t + hw, seq) - jnp.maximum(t - hw, 0)).astype(F32)
        pooled = acc / cnt - ext[h:h + ts, c0:c0 + POOL_GROUP]
        m = jnp.dot(pooled.astype(BF16), pw_ref[g], preferred_element_type=F32)
        mix[:, c0:c0 + POOL_GROUP] = m * ps_ref[:, c0:c0 + POOL_GROUP]
    mm = mix[...]
    ms = jnp.mean(mm * mm, axis=-1, keepdims=True)
    o_ref[0] = (mm * lax.rsqrt(ms + EPS) * gn_ref[...]).astype(BF16)


def _pool(u3d, pool_w_bf16, pool_scale, gn_pool, ts):
    B, S, _ = u3d.shape
    hb = ts // POOL_HALO
    n_hblk = S // POOL_HALO
    return pl.pallas_call(
        functools.partial(_pool_kernel, seq=S),
        grid=(B, S // ts),
        in_specs=[pl.BlockSpec((1, POOL_HALO, POOL_WIDTH), lambda b, s: (b, jnp.maximum(s * hb - 1, 0), 0)),
                  pl.BlockSpec((1, ts, POOL_WIDTH), lambda b, s: (b, s, 0)),
                  pl.BlockSpec((1, POOL_HALO, POOL_WIDTH),
                               lambda b, s: (b, jnp.minimum((s + 1) * hb, n_hblk - 1), 0)),
                  pl.BlockSpec((len(POOL_WINDOWS), POOL_GROUP, POOL_GROUP), lambda b, s: (0, 0, 0)),
                  pl.BlockSpec((1, POOL_WIDTH), lambda b, s: (0, 0)),
                  pl.BlockSpec((1, POOL_WIDTH), lambda b, s: (0, 0))],
        out_specs=pl.BlockSpec((1, ts, POOL_WIDTH), lambda b, s: (b, s, 0)),
        out_shape=jax.ShapeDtypeStruct((B, S, POOL_WIDTH), BF16),
        scratch_shapes=[pltpu.VMEM((ts + 2 * POOL_HALO, POOL_WIDTH), F32),
                        pltpu.VMEM((ts, POOL_WIDTH), F32)],
        compiler_params=_params(2),
        name="pool_mixer",
    )(u3d, u3d, u3d, pool_w_bf16, pool_scale, gn_pool)


ATTN_BLOCKS_PER_STEP = 2


def _attn_kernel(q_ref, kvp_ref, kvc_ref, kvn_ref, bias_ref, sink_ref, gn_ref, o_ref, att):
    nsub = ATTN_BLOCKS_PER_STEP
    n = pl.program_id(1)
    nb = pl.num_programs(1) * nsub
    scale = HEAD_DIM ** -0.5 * LOG2E
    col = lax.broadcasted_iota(jnp.int32, (1, 3 * BLOCK), 1)
    head = lax.broadcasted_iota(jnp.int32, (Q_PER_KV * BLOCK, 1), 0) // BLOCK

    def kv_block(j, cols):
        if j == 0:
            return kvp_ref[0, :, cols]
        if j == nsub + 1:
            return kvn_ref[0, :, cols]
        return kvc_ref[0, (j - 1) * BLOCK:j * BLOCK, cols]

    for sub in range(nsub):
        blk = n * nsub + sub
        rows = slice(sub * BLOCK, (sub + 1) * BLOCK)
        col_ok = jnp.logical_and(jnp.logical_or(blk > 0, col >= BLOCK),
                                 jnp.logical_or(blk < nb - 1, col < 2 * BLOCK))
        col_mask = jnp.where(col_ok, 0.0, MASKED)
        for k in range(N_KV_HEADS):
            ks = slice(k * HEAD_DIM, (k + 1) * HEAD_DIM)
            vs = slice((N_KV_HEADS + k) * HEAD_DIM, (N_KV_HEADS + k + 1) * HEAD_DIM)
            kk = jnp.concatenate([kv_block(sub + j, ks) for j in range(3)], axis=0)
            vv = jnp.concatenate([kv_block(sub + j, vs) for j in range(3)], axis=0)
            h0 = k * Q_PER_KV
            qk = jnp.concatenate([q_ref[0, rows, (h0 + g) * HEAD_DIM:(h0 + g + 1) * HEAD_DIM]
                                  for g in range(Q_PER_KV)], axis=0)
            logits = lax.dot_general(qk, kk, (((1,), (1,)), ((), ())), preferred_element_type=F32)
            bias = bias_ref[h0:h0 + Q_PER_KV].reshape(Q_PER_KV * BLOCK, 3 * BLOCK)
            logits = logits * scale + bias + col_mask
            sink = jnp.full((Q_PER_KV * BLOCK, 1), sink_ref[h0] * LOG2E, F32)
            for g in range(1, Q_PER_KV):
                sink = jnp.where(head == g, sink_ref[h0 + g] * LOG2E, sink)
            m = jnp.maximum(jnp.max(logits, axis=-1, keepdims=True), sink)
            p = jnp.exp2(logits - m)
            denom = jnp.sum(p, axis=-1, keepdims=True) + jnp.exp2(sink - m)
            o = jnp.dot(p.astype(BF16), vv, preferred_element_type=F32) / denom
            for g in range(Q_PER_KV):
                att[rows, (h0 + g) * HEAD_DIM:(h0 + g + 1) * HEAD_DIM] = o[g * BLOCK:(g + 1) * BLOCK]
    a = att[...]
    ms = jnp.mean(a * a, axis=-1, keepdims=True)
    o_ref[0] = (a * lax.rsqrt(ms + EPS) * gn_ref[...]).astype(BF16)


def _attention(q3d, kv3d, bias_tbl, sink, gn_attn):
    B, S, _ = q3d.shape
    nb = S // BLOCK
    nsub = ATTN_BLOCKS_PER_STEP
    assert nb % nsub == 0
    rows = nsub * BLOCK
    kv_spec = lambda f: pl.BlockSpec((1, BLOCK, KV_WIDTH), f)
    return pl.pallas_call(
        _attn_kernel,
        grid=(B, nb // nsub),
        in_specs=[pl.BlockSpec((1, rows, ATTN_WIDTH), lambda b, n: (b, n, 0)),
                  kv_spec(lambda b, n: (b, jnp.maximum(n * nsub - 1, 0), 0)),
                  pl.BlockSpec((1, rows, KV_WIDTH), lambda b, n: (b, n, 0)),
                  kv_spec(lambda b, n: (b, jnp.minimum((n + 1) * nsub, nb - 1), 0)),
                  pl.BlockSpec((N_HEADS, BLOCK, 3 * BLOCK), lambda b, n: (0, 0, 0)),
                  pl.BlockSpec(memory_space=pltpu.SMEM),
                  pl.BlockSpec((1, ATTN_WIDTH), lambda b, n: (0, 0))],
        out_specs=pl.BlockSpec((1, rows, ATTN_WIDTH), lambda b, n: (b, n, 0)),
        out_shape=jax.ShapeDtypeStruct((B, S, ATTN_WIDTH), BF16),
        scratch_shapes=[pltpu.VMEM((rows, ATTN_WIDTH), F32)],
        compiler_params=_params(2),
        name="window_attn",
    )(q3d, kv3d, kv3d, kv3d, bias_tbl, sink, gn_attn)


OUT_CHUNK = 1024


def _outproj_kernel(yp_ref, ya_ref, x_ref, wo_ref, g2_ref, wrt_ref, x1_ref, h2_ref, aff_ref, mix):
    tm = x_ref.shape[0]
    mix[:, 0:POOL_WIDTH] = yp_ref[...]
    mix[:, POOL_WIDTH:] = ya_ref[...]
    ss = jnp.zeros((tm, 1), F32)
    for n0 in range(0, D_MODEL, OUT_CHUNK):
        ns = slice(n0, n0 + OUT_CHUNK)
        x1 = x_ref[:, ns] + jnp.dot(mix[...], wo_ref[:, ns], preferred_element_type=F32)
        x1_ref[:, ns] = x1
        ss = ss + jnp.sum(x1 * x1, axis=-1, keepdims=True)
    inv = lax.rsqrt(ss / D_MODEL + EPS)
    h2 = x1_ref[...] * inv * g2_ref[...]
    h2_ref[...] = h2
    logits_t = lax.dot_general(wrt_ref[...], h2.astype(BF16), (((1,), (1,)), ((), ())),
                               preferred_element_type=F32)
    mx = jnp.max(logits_t, axis=0, keepdims=True)
    ex = jnp.exp(logits_t - mx)
    aff_ref[0] = ex / jnp.sum(ex, axis=0, keepdims=True)


def _outproj(yp2d, ya2d, x2d, wo_bf16, g2, wr_t, tm, B, S):
    T = x2d.shape[0]
    per_b = S // tm
    return pl.pallas_call(
        _outproj_kernel,
        grid=(T // tm,),
        in_specs=[pl.BlockSpec((tm, POOL_WIDTH), lambda i: (i, 0)),
                  pl.BlockSpec((tm, ATTN_WIDTH), lambda i: (i, 0)),
                  pl.BlockSpec((tm, D_MODEL), lambda i: (i, 0)),
                  pl.BlockSpec((D_MODEL, D_MODEL), lambda i: (0, 0)),
                  pl.BlockSpec((1, D_MODEL), lambda i: (0, 0)),
                  pl.BlockSpec((N_EXPERTS, D_MODEL), lambda i: (0, 0))],
        out_specs=[pl.BlockSpec((tm, D_MODEL), lambda i: (i, 0)),
                   pl.BlockSpec((tm, D_MODEL), lambda i: (i, 0)),
                   pl.BlockSpec((1, N_EXPERTS, tm), lambda i: (i // per_b, 0, i % per_b))],
        out_shape=[jax.ShapeDtypeStruct((T, D_MODEL), F32),
                   jax.ShapeDtypeStruct((T, D_MODEL), F32),
                   jax.ShapeDtypeStruct((B, N_EXPERTS, S), F32)],
        scratch_shapes=[pltpu.VMEM((tm, POOL_WIDTH + ATTN_WIDTH), BF16)],
        compiler_params=_params(1),
        name="outproj_norm2_router",
    )(yp2d, ya2d, x2d, wo_bf16, g2, wr_t)


def _route_kernel(aff_ref, rows_ref, gate_ref, posm, bounds_vmem, bounds_smem, bounds_sem, *, cap, slot_blk):
    b = pl.program_id(0)
    S = aff_ref.shape[2]
    v = aff_ref[0]
    bits = pltpu.bitcast(v, jnp.int32)

    def count(mask):
        return jnp.sum(jnp.where(mask, 1.0, 0.0), axis=1, keepdims=True)

    thr = jnp.zeros((N_EXPERTS, 1), jnp.int32)
    for bit in range(30, -1, -1):
        cand = thr | (1 << bit)
        thr = jnp.where(count(bits >= cand) >= cap, cand, thr)
    gt = bits > thr
    eq = bits == thr
    need = cap - count(gt)
    tok = lax.broadcasted_iota(jnp.int32, (N_EXPERTS, S), 1)
    bound = jnp.zeros((N_EXPERTS, 1), jnp.int32)
    for bit in range(S.bit_length() - 1, -1, -1):
        cand = bound | (1 << bit)
        bound = jnp.where(count(jnp.logical_and(eq, tok < cand)) <= need, cand, bound)
    sel = jnp.where(jnp.logical_or(gt, jnp.logical_and(eq, tok < bound)), 1.0, 0.0)

    tri = jnp.where(lax.broadcasted_iota(jnp.int32, (LANES, LANES), 0)
                    < lax.broadcasted_iota(jnp.int32, (LANES, LANES), 1), 1.0, 0.0).astype(BF16)
    lane = lax.broadcasted_iota(jnp.int32, (1, LANES), 1)
    n_chunks = S // LANES
    assert n_chunks <= LANES and 2 * (cap // slot_blk) <= LANES
    off = jnp.zeros((N_EXPERTS, 1), F32)
    slots_before = jnp.full((N_EXPERTS, LANES), float(S), F32)
    slots_through = jnp.full((N_EXPERTS, LANES), float(S), F32)
    for c in range(n_chunks):
        sc = sel[:, c * LANES:(c + 1) * LANES]
        within = jnp.dot(sc.astype(BF16), tri, preferred_element_type=F32)
        posm[:, c * LANES:(c + 1) * LANES] = jnp.where(sc > 0.0, within + off, -1.0)
        slots_before = jnp.where(lane == c, off, slots_before)
        off = off + jnp.sum(sc, axis=1, keepdims=True)
        slots_through = jnp.where(lane == c, off, slots_through)

    bounds = jnp.zeros((N_EXPERTS, LANES), jnp.int32)
    for jb in range(cap // slot_blk):
        first = count(slots_through <= float(jb * slot_blk)).astype(jnp.int32)
        last = count(slots_before < float((jb + 1) * slot_blk)).astype(jnp.int32)
        bounds = jnp.where(lane == 2 * jb, first, bounds)
        bounds = jnp.where(lane == 2 * jb + 1, last, bounds)
    bounds_vmem[...] = bounds
    to_smem = pltpu.make_async_copy(bounds_vmem, bounds_smem, bounds_sem)
    to_smem.start()
    to_smem.wait()

    rows_ref[...] = jnp.zeros(rows_ref.shape, jnp.int32)
    gate_ref[...] = jnp.zeros(gate_ref.shape, F32)
    for e in range(N_EXPERTS):
        for jb in range(cap // slot_blk):
            slot = (lax.broadcasted_iota(jnp.int32, (slot_blk, 1), 0) + jb * slot_blk).astype(F32)

            def body(c, carry):
                acc_row, acc_gate = carry
                o = pl.multiple_of(c * LANES, LANES)
                hit = posm[e:e + 1, pl.ds(o, LANES)] == slot
                rowid = (lane + (c * LANES + b * S)).astype(F32)
                acc_row = jnp.where(hit, rowid, acc_row)
                acc_gate = jnp.where(hit, aff_ref[0, e:e + 1, pl.ds(o, LANES)], acc_gate)
                return acc_row, acc_gate

            zero = jnp.zeros((slot_blk, LANES), F32)
            acc_row, acc_gate = lax.fori_loop(bounds_smem[e, 2 * jb], bounds_smem[e, 2 * jb + 1], body,
                                              (zero, zero))
            js = slice(jb * slot_blk, (jb + 1) * slot_blk)
            rows_ref[0, js, e:e + 1] = jnp.sum(acc_row, axis=1, keepdims=True).astype(jnp.int32)
            gate_ref[0, js, e:e + 1] = jnp.sum(acc_gate, axis=1, keepdims=True)


def _route(aff_t, cap):
    B, _, S = aff_t.shape
    slot_blk = min(LANES, cap)
    return pl.pallas_call(
        functools.partial(_route_kernel, cap=cap, slot_blk=slot_blk),
        grid=(B,),
        in_specs=[pl.BlockSpec((1, N_EXPERTS, S), lambda b: (b, 0, 0))],
        out_specs=[pl.BlockSpec((1, cap, LANES), lambda b: (b, 0, 0)),
                   pl.BlockSpec((1, cap, LANES), lambda b: (b, 0, 0))],
        out_shape=[jax.ShapeDtypeStruct((B, cap, LANES), jnp.int32),
                   jax.ShapeDtypeStruct((B, cap, LANES), F32)],
        scratch_shapes=[pltpu.VMEM((N_EXPERTS, S), F32),
                        pltpu.VMEM((N_EXPERTS, LANES), jnp.int32),
                        pltpu.SMEM((N_EXPERTS, LANES), jnp.int32),
                        pltpu.SemaphoreType.DMA(())],
        compiler_params=_params(1),
        name="expert_choice_route",
    )(aff_t)


FF_TILE = 256
N_FF_STEPS = D_FF // FF_TILE


def _ffn_kernel(rows_ref, rows_nxt_ref, gate_ref, h2_hbm, wg_hbm, wu_hbm, wd_hbm, xin_hbm, out_hbm,
                xs, yacc, gcol, stage_x, stage_b, wgb, wub, wdb, sem_x, sem_b, sem_s, sem_w, *, n_rows, m_chunk):
    del xin_hbm
    e = pl.program_id(0)
    ne = pl.num_programs(0)
    nf = N_FF_STEPS
    chunk = n_rows // nf
    n_m = n_rows // m_chunk
    per_m = chunk // n_m
    cur = lax.rem(e, 2)

    def x_copy(rows_smem, c, r):
        row = rows_smem[0, 0, c * chunk + r]
        return pltpu.make_async_copy(h2_hbm.at[pl.ds(row, 1)], stage_x.at[pl.ds(r, 1)], sem_x)

    def b_copy(rows_smem, c, r):
        row = rows_smem[0, 0, c * chunk + r]
        return pltpu.make_async_copy(out_hbm.at[pl.ds(row, 1)], stage_b.at[pl.ds(r, 1)], sem_b)

    def s_copy(r):
        row = rows_ref[0, 0, r]
        return pltpu.make_async_copy(yacc.at[pl.ds(r, 1)], out_hbm.at[pl.ds(row, 1)], sem_s)

    def wait_x():
        pltpu.make_async_copy(h2_hbm.at[pl.ds(0, chunk)], stage_x, sem_x).wait()

    def wait_b():
        pltpu.make_async_copy(out_hbm.at[pl.ds(0, chunk)], stage_b, sem_b).wait()

    def wait_s(n):
        pltpu.make_async_copy(yacc.at[pl.ds(0, n)], out_hbm.at[pl.ds(0, n)], sem_s).wait()

    def w_copies(ee, ff, slot):
        fo = pl.multiple_of(ff * FF_TILE, FF_TILE)
        return (pltpu.make_async_copy(wg_hbm.at[ee, :, pl.ds(fo, FF_TILE)], wgb.at[slot], sem_w.at[slot]),
                pltpu.make_async_copy(wu_hbm.at[ee, :, pl.ds(fo, FF_TILE)], wub.at[slot], sem_w.at[slot]),
                pltpu.make_async_copy(wd_hbm.at[ee, pl.ds(fo, FF_TILE), :], wdb.at[slot], sem_w.at[slot]))

    def w_start(ee, ff, slot):
        for cp in w_copies(ee, ff, slot):
            cp.start(priority=1)

    def loop_start(copy_fn, n):
        def body(r, carry):
            copy_fn(r).start()
            return carry
        lax.fori_loop(0, n, body, 0, unroll=8)

    @pl.when(e == 0)
    def _first_expert_rows():
        for c in range(nf - 1):
            loop_start(lambda r, c=c: x_copy(rows_ref, c, r), chunk)
            wait_x()
            xs[0, c * chunk:(c + 1) * chunk, :] = stage_x[...].astype(BF16)
        loop_start(lambda r: x_copy(rows_ref, nf - 1, r), chunk)
        loop_start(lambda r: b_copy(rows_ref, 0, r), chunk)
        w_start(0, 0, 0)

    lane = lax.broadcasted_iota(jnp.int32, (1, 1, LANES), 2)
    g = jnp.sum(jnp.where(lane == e, gate_ref[...], 0.0), axis=-1, keepdims=True)
    gcol[...] = g.reshape(n_rows, 1)

    def split3(rs):
        k = -(-len(rs) // 3)
        return rs[:k], rs[k:2 * k], rs[2 * k:]

    def step(f, first, last):
        pc = lax.rem(f + (nf - 1), nf)
        tgt = lax.rem(e + (0 if first else 1), 2)
        wait_x()
        xs[tgt, pl.ds(pl.multiple_of(pc * chunk, chunk), chunk), :] = stage_x[...].astype(BF16)
        if not first:
            wait_b()
            yacc[pl.ds(pl.multiple_of(f * chunk, chunk), chunk), :] += stage_b[...]

        wslot = lax.rem(f, 2)
        for cp in w_copies(e, f, wslot):
            cp.wait()
        if not last:
            w_start(e, f + 1, 1 - wslot)
        else:
            @pl.when(e < ne - 1)
            def _next_expert_first_tiles():
                w_start(e + 1, 0, 1 - wslot)
        wg = wgb[wslot].astype(BF16)
        wu = wub[wslot].astype(BF16)
        wd = wdb[wslot].astype(BF16)

        x_by_m = [list(range(m * per_m, (m + 1) * per_m)) for m in range(n_m)]
        all_b = list(range(chunk))
        b_tail = []
        if last:
            b_by_m = [[] for _ in range(n_m)]
        elif not first:
            b_by_m = x_by_m
        elif n_m > 1:
            per = -(-chunk // (n_m - 1))
            b_by_m = [[]] + [all_b[i * per:(i + 1) * per] for i in range(n_m - 1)]
        else:
            b_by_m, b_tail = [[]], all_b

        def start_gathers(xr, br):
            for r in xr:
                x_copy(rows_nxt_ref, f, r).start()
            for r in br:
                b_copy(rows_ref, f + 1, r).start(priority=1)

        for m in range(n_m):
            xr, br = split3(x_by_m[m]), split3(b_by_m[m])
            ms = slice(m * m_chunk, (m + 1) * m_chunk)
            start_gathers(xr[0], br[0])
            xm = xs[cur, ms, :]
            a = jnp.dot(xm, wg, preferred_element_type=F32)
            start_gathers(xr[1], br[1])
            u = jnp.dot(xm, wu, preferred_element_type=F32)
            hm = (a * (1.0 / (1.0 + jnp.exp(-a))) * u).astype(BF16)
            start_gathers(xr[2], br[2])
            y = gcol[ms, :] * jnp.dot(hm, wd, preferred_element_type=F32)
            if first:
                yacc[ms, :] = y
                if m == 0:
                    wait_b()
                    yacc[0:chunk, :] += stage_b[...]
                    start_gathers([], b_tail)
            else:
                yacc[ms, :] += y
            if last:
                for r in range(m * m_chunk, (m + 1) * m_chunk):
                    s_copy(r).start()
        if last:
            for m in range(n_m):
                wait_s(m_chunk)

            @pl.when(e < ne - 1)
            def _next_expert_first_residual_chunk():
                loop_start(lambda r: b_copy(rows_nxt_ref, 0, r), chunk)

            @pl.when(e == ne - 1)
            def _drain_unused_prefetch():
                wait_x()

    step(jnp.int32(0), True, False)

    def mid_step(f, carry):
        step(f, False, False)
        return carry
    lax.fori_loop(1, nf - 1, mid_step, 0)
    step(jnp.int32(nf - 1), False, True)


def _expert_ffn(rows, gate_col, h2, w_gate, w_up, w_down, x1):
    T = x1.shape[0]
    n_rows = rows.shape[-1]
    B, cap, _ = gate_col.shape
    chunk = n_rows // N_FF_STEPS
    m_chunk = min(1024, n_rows)
    assert chunk % 8 == 0 and chunk % (n_rows // m_chunk) == 0
    rows_spec = lambda f: pl.BlockSpec((1, 1, n_rows), f, memory_space=pltpu.SMEM)
    return pl.pallas_call(
        functools.partial(_ffn_kernel, n_rows=n_rows, m_chunk=m_chunk),
        grid=(N_EXPERTS,),
        in_specs=[rows_spec(lambda e: (e, 0, 0)),
                  rows_spec(lambda e: (jnp.minimum(e + 1, N_EXPERTS - 1), 0, 0)),
                  pl.BlockSpec((B, cap, LANES), lambda e: (0, 0, 0)),
                  pl.BlockSpec(memory_space=pl.ANY),
                  pl.BlockSpec(memory_space=pl.ANY),
                  pl.BlockSpec(memory_space=pl.ANY),
                  pl.BlockSpec(memory_space=pl.ANY),
                  pl.BlockSpec(memory_space=pl.ANY)],
        out_specs=pl.BlockSpec(memory_space=pl.ANY),
        out_shape=jax.ShapeDtypeStruct((T, D_MODEL), F32),
        input_output_aliases={7: 0},
        scratch_shapes=[pltpu.VMEM((2, n_rows, D_MODEL), BF16),
                        pltpu.VMEM((n_rows, D_MODEL), F32),
                        pltpu.VMEM((n_rows, 1), F32),
                        pltpu.VMEM((chunk, D_MODEL), F32),
                        pltpu.VMEM((chunk, D_MODEL), F32),
                        pltpu.VMEM((2, D_MODEL, FF_TILE), F32),
                        pltpu.VMEM((2, D_MODEL, FF_TILE), F32),
                        pltpu.VMEM((2, FF_TILE, D_MODEL), F32),
                        pltpu.SemaphoreType.DMA(()),
                        pltpu.SemaphoreType.DMA(()),
                        pltpu.SemaphoreType.DMA(()),
                        pltpu.SemaphoreType.DMA((2,))],
        compiler_params=_params(1),
        name="expert_ffn",
    )(rows, rows, gate_col, h2, w_gate, w_up, w_down, x1)


def _final_norm_kernel(x_ref, g_ref, o_ref):
    xf = x_ref[...]
    ms = jnp.mean(xf * xf, axis=-1, keepdims=True)
    o_ref[...] = xf * lax.rsqrt(ms + EPS) * g_ref[...]


def _final_norm(x2d, g, tm):
    T = x2d.shape[0]
    return pl.pallas_call(
        _final_norm_kernel,
        grid=(T // tm,),
        in_specs=[pl.BlockSpec((tm, D_MODEL), lambda i: (i, 0)),
                  pl.BlockSpec((1, D_MODEL), lambda i: (0, 0))],
        out_specs=pl.BlockSpec((tm, D_MODEL), lambda i: (i, 0)),
        out_shape=jax.ShapeDtypeStruct((T, D_MODEL), F32),
        compiler_params=_params(1),
        name="final_norm",
    )(x2d, g)


def kernel(x, norm1_g, w_in, pool_w, pool_scale, rel_bias, sink, gn_pool, gn_attn, w_out, norm2_g,
           w_router, w_gate, w_up, w_down, final_g):
    B, S, D = x.shape
    assert D == D_MODEL and S % BLOCK == 0 and norm1_g.shape[0] == 1
    T = B * S
    cap = CAPACITY_FACTOR * S // N_EXPERTS
    tm = min(512, S)
    row = lambda a: a.reshape(1, -1)

    bias_tbl = _bias_table(rel_bias)
    x2d = x.reshape(T, D)
    u, q, kv = _inproj(x2d, row(norm1_g[0]), w_in[0].astype(BF16), tm)
    y_pool = _pool(u.reshape(B, S, POOL_WIDTH), pool_w[0].astype(BF16), row(pool_scale[0]),
                   row(gn_pool[0]), tm)
    y_attn = _attention(q.reshape(B, S, ATTN_WIDTH), kv.reshape(B, S, KV_WIDTH), bias_tbl, sink[0],
                        row(gn_attn[0]))
    x1, h2, aff_t = _outproj(y_pool.reshape(T, POOL_WIDTH), y_attn.reshape(T, ATTN_WIDTH), x2d,
                             w_out[0].astype(BF16), row(norm2_g[0]), w_router[0].T.astype(BF16), tm, B, S)
    rows_col, gate_col = _route(aff_t, cap)
    rows = rows_col[:, :, :N_EXPERTS].transpose(2, 0, 1).reshape(N_EXPERTS, 1, B * cap)
    x2 = _expert_ffn(rows, gate_col, h2, w_gate[0], w_up[0], w_down[0], x1)
    return _final_norm(x2, row(final_g), tm).reshape(B, S, D)
```

```python
import functools
import math

import numpy as np
import jax
import jax.numpy as jnp
from jax import lax
from jax.experimental import pallas as pl
from jax.experimental.pallas import tpu as pltpu

D_MODEL = 2048
POOL_WIDTH = 1024
POOL_WINDOWS = (2, 4, 8, 16)
POOL_GROUP = POOL_WIDTH // len(POOL_WINDOWS)
POOL_HALO = 8
ATTN_WIDTH = 1024
HEAD_DIM = 128
N_HEADS = ATTN_WIDTH // HEAD_DIM
N_KV_HEADS = 2
Q_PER_KV = N_HEADS // N_KV_HEADS
KV_WIDTH = 2 * N_KV_HEADS * HEAD_DIM
WINDOW = 128
BLOCK = 128
N_BUCKETS = 32
MAX_DISTANCE = 128
N_EXPERTS = 16
CAPACITY_FACTOR = 2
D_FF = 2 * D_MODEL
IN_WIDTH = POOL_WIDTH + ATTN_WIDTH + KV_WIDTH
EPS = 1e-6
MASKED = -1e30
LOG2E = math.log2(math.e)
LANES = 128
SUBLANES = 8
VMEM_LIMIT_BYTES = 58 * 1024 * 1024

F32 = jnp.float32
BF16 = jnp.bfloat16


def _params(n_axes):
    return pltpu.CompilerParams(dimension_semantics=("arbitrary",) * n_axes,
                                vmem_limit_bytes=VMEM_LIMIT_BYTES)


def _t5_bucket_table():
    half = N_BUCKETS // 2
    max_exact = half // 2
    qi = np.arange(BLOCK)[:, None]
    kj = np.arange(3 * BLOCK)[None, :] - BLOCK
    rel = kj - qi
    n = np.abs(rel)
    nf = np.maximum(n, 1).astype(np.float64)
    large = max_exact + np.floor(np.log(nf / max_exact) / math.log(MAX_DISTANCE / max_exact)
                                 * (half - max_exact) + 1e-9).astype(np.int64)
    large = np.minimum(large, half - 1)
    bucket = np.where(rel > 0, half, 0) + np.where(n < max_exact, n, large)
    return np.where(n <= WINDOW, bucket, -1).astype(np.int32)


def _bias_kernel(bucket_ref, relb_ref, o_ref):
    bk = bucket_ref[...]
    for h in range(N_HEADS):
        acc = jnp.full(bk.shape, MASKED, F32)
        for b in range(N_BUCKETS):
            acc = jnp.where(bk == b, relb_ref[b, h] * LOG2E, acc)
        o_ref[h] = acc


def _bias_table(rel_bias):
    bucket = jnp.asarray(_t5_bucket_table())
    return pl.pallas_call(
        _bias_kernel,
        out_shape=jax.ShapeDtypeStruct((N_HEADS, BLOCK, 3 * BLOCK), F32),
        in_specs=[pl.BlockSpec(memory_space=pltpu.VMEM), pl.BlockSpec(memory_space=pltpu.SMEM)],
        out_specs=pl.BlockSpec(memory_space=pltpu.VMEM),
        name="bias_table",
    )(bucket, rel_bias)


def _inproj_kernel(x_ref, g_ref, w_ref, u_ref, q_ref, kv_ref, h_scr):
    xf = x_ref[...]
    ms = jnp.mean(xf * xf, axis=-1, keepdims=True)
    h_scr[...] = (xf * lax.rsqrt(ms + EPS) * g_ref[...]).astype(BF16)
    q0, kv0 = POOL_WIDTH, POOL_WIDTH + ATTN_WIDTH
    u_ref[...] = jnp.dot(h_scr[...], w_ref[:, :q0], preferred_element_type=F32)
    q_ref[...] = jnp.dot(h_scr[...], w_ref[:, q0:kv0], preferred_element_type=F32).astype(BF16)
    kv_ref[...] = jnp.dot(h_scr[...], w_ref[:, kv0:], preferred_element_type=F32).astype(BF16)


def _inproj(x2d, g, w_bf16, tm):
    T = x2d.shape[0]
    return pl.pallas_call(
        _inproj_kernel,
        grid=(T // tm,),
        in_specs=[pl.BlockSpec((tm, D_MODEL), lambda i: (i, 0)),
                  pl.BlockSpec((1, D_MODEL), lambda i: (0, 0)),
                  pl.BlockSpec((D_MODEL, IN_WIDTH), lambda i: (0, 0))],
        out_specs=[pl.BlockSpec((tm, POOL_WIDTH), lambda i: (i, 0)),
                   pl.BlockSpec((tm, ATTN_WIDTH), lambda i: (i, 0)),
                   pl.BlockSpec((tm, KV_WIDTH), lambda i: (i, 0))],
        out_shape=[jax.ShapeDtypeStruct((T, POOL_WIDTH), F32),
                   jax.ShapeDtypeStruct((T, ATTN_WIDTH), BF16),
                   jax.ShapeDtypeStruct((T, KV_WIDTH), BF16)],
        scratch_shapes=[pltpu.VMEM((tm, D_MODEL), BF16)],
        compiler_params=_params(1),
        name="norm1_inproj",
    )(x2d, g, w_bf16)


def _pool_kernel(up_ref, uc_ref, un_ref, pw_ref, ps_ref, gn_ref, o_ref, ext, mix, *, seq):
    s = pl.program_id(1)
    ns = pl.num_programs(1)
    ts = uc_ref.shape[1]
    h = POOL_HALO
    ext[0:h, :] = jnp.where(s > 0, up_ref[0], 0.0)
    ext[h:h + ts, :] = uc_ref[0]
    ext[h + ts:2 * h + ts, :] = jnp.where(s < ns - 1, un_ref[0], 0.0)
    t = s * ts + lax.broadcasted_iota(jnp.int32, (ts, 1), 0)
    for g, w in enumerate(POOL_WINDOWS):
        hw = w // 2
        c0 = g * POOL_GROUP
        acc = ext[h - hw:h - hw + ts, c0:c0 + POOL_GROUP]
        for d in range(-hw + 1, hw):
            acc = acc + ext[h + d:h + d + ts, c0:c0 + POOL_GROUP]
        cnt = (jnp.minimum(t + hw, seq) - jnp.maximum(t - hw, 0)).astype(F32)
        pooled = acc / cnt - ext[h:h + ts, c0:c0 + POOL_GROUP]
        m = jnp.dot(pooled.astype(BF16), pw_ref[g], preferred_element_type=F32)
        mix[:, c0:c0 + POOL_GROUP] = m * ps_ref[:, c0:c0 + POOL_GROUP]
    mm = mix[...]
    ms = jnp.mean(mm * mm, axis=-1, keepdims=True)
    o_ref[0] = (mm * lax.rsqrt(ms + EPS) * gn_ref[...]).astype(BF16)


def _pool(u3d, pool_w_bf16, pool_scale, gn_pool, ts):
    B, S, _ = u3d.shape
    hb = ts // POOL_HALO
    n_hblk = S // POOL_HALO
    return pl.pallas_call(
        functools.partial(_pool_kernel, seq=S),
        grid=(B, S // ts),
        in_specs=[pl.BlockSpec((1, POOL_HALO, POOL_WIDTH), lambda b, s: (b, jnp.maximum(s * hb - 1, 0), 0)),
                  pl.BlockSpec((1, ts, POOL_WIDTH), lambda b, s: (b, s, 0)),
                  pl.BlockSpec((1, POOL_HALO, POOL_WIDTH),
                               lambda b, s: (b, jnp.minimum((s + 1) * hb, n_hblk - 1), 0)),
                  pl.BlockSpec((len(POOL_WINDOWS), POOL_GROUP, POOL_GROUP), lambda b, s: (0, 0, 0)),
                  pl.BlockSpec((1, POOL_WIDTH), lambda b, s: (0, 0)),
                  pl.BlockSpec((1, POOL_WIDTH), lambda b, s: (0, 0))],
        out_specs=pl.BlockSpec((1, ts, POOL_WIDTH), lambda b, s: (b, s, 0)),
        out_shape=jax.ShapeDtypeStruct((B, S, POOL_WIDTH), BF16),
        scratch_shapes=[pltpu.VMEM((ts + 2 * POOL_HALO, POOL_WIDTH), F32),
                        pltpu.VMEM((ts, POOL_WIDTH), F32)],
        compiler_params=_params(2),
        name="pool_mixer",
    )(u3d, u3d, u3d, pool_w_bf16, pool_scale, gn_pool)


ATTN_BLOCKS_PER_STEP = 4


def _attn_kernel(q_ref, kvp_ref, kvc_ref, kvn_ref, bias_ref, sink_ref, gn_ref, o_ref, att):
    nsub = ATTN_BLOCKS_PER_STEP
    n = pl.program_id(1)
    nb = pl.num_programs(1) * nsub
    scale = HEAD_DIM ** -0.5 * LOG2E
    col = lax.broadcasted_iota(jnp.int32, (1, 3 * BLOCK), 1)
    head = lax.broadcasted_iota(jnp.int32, (Q_PER_KV * BLOCK, 1), 0) // BLOCK

    def kv_block(j, cols):
        if j == 0:
            return kvp_ref[0, :, cols]
        if j == nsub + 1:
            return kvn_ref[0, :, cols]
        return kvc_ref[0, (j - 1) * BLOCK:j * BLOCK, cols]

    for sub in range(nsub):
        blk = n * nsub + sub
        rows = slice(sub * BLOCK, (sub + 1) * BLOCK)
        col_ok = jnp.logical_and(jnp.logical_or(blk > 0, col >= BLOCK),
                                 jnp.logical_or(blk < nb - 1, col < 2 * BLOCK))
        col_mask = jnp.where(col_ok, 0.0, MASKED)
        for k in range(N_KV_HEADS):
            ks = slice(k * HEAD_DIM, (k + 1) * HEAD_DIM)
            vs = slice((N_KV_HEADS + k) * HEAD_DIM, (N_KV_HEADS + k + 1) * HEAD_DIM)
            kk = jnp.concatenate([kv_block(sub + j, ks) for j in range(3)], axis=0)
            vv = jnp.concatenate([kv_block(sub + j, vs) for j in range(3)], axis=0)
            h0 = k * Q_PER_KV
            qk = jnp.concatenate([q_ref[0, rows, (h0 + g) * HEAD_DIM:(h0 + g + 1) * HEAD_DIM]
                                  for g in range(Q_PER_KV)], axis=0)
            logits = lax.dot_general(qk, kk, (((1,), (1,)), ((), ())), preferred_element_type=F32)
            bias = bias_ref[h0:h0 + Q_PER_KV].reshape(Q_PER_KV * BLOCK, 3 * BLOCK)
            logits = logits * scale + bias + col_mask
            sink = jnp.full((Q_PER_KV * BLOCK, 1), sink_ref[h0] * LOG2E, F32)
            for g in range(1, Q_PER_KV):
                sink = jnp.where(head == g, sink_ref[h0 + g] * LOG2E, sink)
            m = jnp.maximum(jnp.max(logits, axis=-1, keepdims=True), sink)
            p = jnp.exp2(logits - m)
            denom = jnp.sum(p, axis=-1, keepdims=True) + jnp.exp2(sink - m)
            o = jnp.dot(p.astype(BF16), vv, preferred_element_type=F32) / denom
            for g in range(Q_PER_KV):
                att[rows, (h0 + g) * HEAD_DIM:(h0 + g + 1) * HEAD_DIM] = o[g * BLOCK:(g + 1) * BLOCK]
    a = att[...]
    ms = jnp.mean(a * a, axis=-1, keepdims=True)
    o_ref[0] = (a * lax.rsqrt(ms + EPS) * gn_ref[...]).astype(BF16)


def _attention(q3d, kv3d, bias_tbl, sink, gn_attn):
    B, S, _ = q3d.shape
    nb = S // BLOCK
    nsub = ATTN_BLOCKS_PER_STEP
    assert nb % nsub == 0
    rows = nsub * BLOCK
    kv_spec = lambda f: pl.BlockSpec((1, BLOCK, KV_WIDTH), f)
    return pl.pallas_call(
        _attn_kernel,
        grid=(B, nb // nsub),
        in_specs=[pl.BlockSpec((1, rows, ATTN_WIDTH), lambda b, n: (b, n, 0)),
                  kv_spec(lambda b, n: (b, jnp.maximum(n * nsub - 1, 0), 0)),
                  pl.BlockSpec((1, rows, KV_WIDTH), lambda b, n: (b, n, 0)),
                  kv_spec(lambda b, n: (b, jnp.minimum((n + 1) * nsub, nb - 1), 0)),
                  pl.BlockSpec((N_HEADS, BLOCK, 3 * BLOCK), lambda b, n: (0, 0, 0)),
                  pl.BlockSpec(memory_space=pltpu.SMEM),
                  pl.BlockSpec((1, ATTN_WIDTH), lambda b, n: (0, 0))],
        out_specs=pl.BlockSpec((1, rows, ATTN_WIDTH), lambda b, n: (b, n, 0)),
        out_shape=jax.ShapeDtypeStruct((B, S, ATTN_WIDTH), BF16),
        scratch_shapes=[pltpu.VMEM((rows, ATTN_WIDTH), F32)],
        compiler_params=_params(2),
        name="window_attn",
    )(q3d, kv3d, kv3d, kv3d, bias_tbl, sink, gn_attn)


OUT_CHUNK = 1024


def _outproj_kernel(yp_ref, ya_ref, x_ref, wo_ref, g2_ref, wrt_ref, x1_ref, h2_ref, aff_ref, mix):
    tm = x_ref.shape[0]
    mix[:, 0:POOL_WIDTH] = yp_ref[...]
    mix[:, POOL_WIDTH:] = ya_ref[...]
    ss = jnp.zeros((tm, 1), F32)
    for n0 in range(0, D_MODEL, OUT_CHUNK):
        ns = slice(n0, n0 + OUT_CHUNK)
        x1 = x_ref[:, ns] + jnp.dot(mix[...], wo_ref[:, ns], preferred_element_type=F32)
        x1_ref[:, ns] = x1
        ss = ss + jnp.sum(x1 * x1, axis=-1, keepdims=True)
    inv = lax.rsqrt(ss / D_MODEL + EPS)
    h2 = x1_ref[...] * inv * g2_ref[...]
    h2_ref[...] = h2
    logits_t = lax.dot_general(wrt_ref[...], h2.astype(BF16), (((1,), (1,)), ((), ())),
                               preferred_element_type=F32)
    mx = jnp.max(logits_t, axis=0, keepdims=True)
    ex = jnp.exp(logits_t - mx)
    aff_ref[0] = ex / jnp.sum(ex, axis=0, keepdims=True)


def _outproj(yp2d, ya2d, x2d, wo_bf16, g2, wr_t, tm, B, S):
    T = x2d.shape[0]
    per_b = S // tm
    return pl.pallas_call(
        _outproj_kernel,
        grid=(T // tm,),
        in_specs=[pl.BlockSpec((tm, POOL_WIDTH), lambda i: (i, 0)),
                  pl.BlockSpec((tm, ATTN_WIDTH), lambda i: (i, 0)),
                  pl.BlockSpec((tm, D_MODEL), lambda i: (i, 0)),
                  pl.BlockSpec((D_MODEL, D_MODEL), lambda i: (0, 0)),
                  pl.BlockSpec((1, D_MODEL), lambda i: (0, 0)),
                  pl.BlockSpec((N_EXPERTS, D_MODEL), lambda i: (0, 0))],
        out_specs=[pl.BlockSpec((tm, D_MODEL), lambda i: (i, 0)),
                   pl.BlockSpec((tm, D_MODEL), lambda i: (i, 0)),
                   pl.BlockSpec((1, N_EXPERTS, tm), lambda i: (i // per_b, 0, i % per_b))],
        out_shape=[jax.ShapeDtypeStruct((T, D_MODEL), F32),
                   jax.ShapeDtypeStruct((T, D_MODEL), F32),
                   jax.ShapeDtypeStruct((B, N_EXPERTS, S), F32)],
        scratch_shapes=[pltpu.VMEM((tm, POOL_WIDTH + ATTN_WIDTH), BF16)],
        compiler_params=_params(1),
        name="outproj_norm2_router",
    )(yp2d, ya2d, x2d, wo_bf16, g2, wr_t)


def _route_kernel(aff_ref, rows_ref, gate_ref, posm, bounds_vmem, bounds_smem, bounds_sem, *, cap, slot_blk):
    b = pl.program_id(0)
    S = aff_ref.shape[2]
    v = aff_ref[0]
    bits = pltpu.bitcast(v, jnp.int32)

    def count(mask):
        return jnp.sum(jnp.where(mask, 1.0, 0.0), axis=1, keepdims=True)

    thr = jnp.zeros((N_EXPERTS, 1), jnp.int32)
    for bit in range(30, -1, -1):
        cand = thr | (1 << bit)
        thr = jnp.where(count(bits >= cand) >= cap, cand, thr)
    gt = bits > thr
    eq = bits == thr
    need = cap - count(gt)
    tok = lax.broadcasted_iota(jnp.int32, (N_EXPERTS, S), 1)
    bound = jnp.zeros((N_EXPERTS, 1), jnp.int32)
    for bit in range(S.bit_length() - 1, -1, -1):
        cand = bound | (1 << bit)
        bound = jnp.where(count(jnp.logical_and(eq, tok < cand)) <= need, cand, bound)
    sel = jnp.where(jnp.logical_or(gt, jnp.logical_and(eq, tok < bound)), 1.0, 0.0)

    tri = jnp.where(lax.broadcasted_iota(jnp.int32, (LANES, LANES), 0)
                    < lax.broadcasted_iota(jnp.int32, (LANES, LANES), 1), 1.0, 0.0).astype(BF16)
    lane = lax.broadcasted_iota(jnp.int32, (1, LANES), 1)
    n_chunks = S // LANES
    assert n_chunks <= LANES and 2 * (cap // slot_blk) <= LANES
    off = jnp.zeros((N_EXPERTS, 1), F32)
    slots_before = jnp.full((N_EXPERTS, LANES), float(S), F32)
    slots_through = jnp.full((N_EXPERTS, LANES), float(S), F32)
    for c in range(n_chunks):
        sc = sel[:, c * LANES:(c + 1) * LANES]
        within = jnp.dot(sc.astype(BF16), tri, preferred_element_type=F32)
        posm[:, c * LANES:(c + 1) * LANES] = jnp.where(sc > 0.0, within + off, -1.0)
        slots_before = jnp.where(lane == c, off, slots_before)
        off = off + jnp.sum(sc, axis=1, keepdims=True)
        slots_through = jnp.where(lane == c, off, slots_through)

    bounds = jnp.zeros((N_EXPERTS, LANES), jnp.int32)
    for jb in range(cap // slot_blk):
        first = count(slots_through <= float(jb * slot_blk)).astype(jnp.int32)
        last = count(slots_before < float((jb + 1) * slot_blk)).astype(jnp.int32)
        bounds = jnp.where(lane == 2 * jb, first, bounds)
        bounds = jnp.where(lane == 2 * jb + 1, last, bounds)
    bounds_vmem[...] = bounds
    to_smem = pltpu.make_async_copy(bounds_vmem, bounds_smem, bounds_sem)
    to_smem.start()
    to_smem.wait()

    rows_ref[...] = jnp.zeros(rows_ref.shape, jnp.int32)
    gate_ref[...] = jnp.zeros(gate_ref.shape, F32)
    for e in range(N_EXPERTS):
        for jb in range(cap // slot_blk):
            slot = (lax.broadcasted_iota(jnp.int32, (slot_blk, 1), 0) + jb * slot_blk).astype(F32)

            def body(c, carry):
                acc_row, acc_gate = carry
                o = pl.multiple_of(c * LANES, LANES)
                hit = posm[e:e + 1, pl.ds(o, LANES)] == slot
                rowid = (lane + (c * LANES + b * S)).astype(F32)
                acc_row = jnp.where(hit, rowid, acc_row)
                acc_gate = jnp.where(hit, aff_ref[0, e:e + 1, pl.ds(o, LANES)], acc_gate)
                return acc_row, acc_gate

            zero = jnp.zeros((slot_blk, LANES), F32)
            acc_row, acc_gate = lax.fori_loop(bounds_smem[e, 2 * jb], bounds_smem[e, 2 * jb + 1], body,
                                              (zero, zero))
            js = slice(jb * slot_blk, (jb + 1) * slot_blk)
            rows_ref[0, js, e:e + 1] = jnp.sum(acc_row, axis=1, keepdims=True).astype(jnp.int32)
            gate_ref[0, js, e:e + 1] = jnp.sum(acc_gate, axis=1, keepdims=True)


def _route(aff_t, cap):
    B, _, S = aff_t.shape
    slot_blk = min(LANES, cap)
    return pl.pallas_call(
        functools.partial(_route_kernel, cap=cap, slot_blk=slot_blk),
        grid=(B,),
        in_specs=[pl.BlockSpec((1, N_EXPERTS, S), lambda b: (b, 0, 0))],
        out_specs=[pl.BlockSpec((1, cap, LANES), lambda b: (b, 0, 0)),
                   pl.BlockSpec((1, cap, LANES), lambda b: (b, 0, 0))],
        out_shape=[jax.ShapeDtypeStruct((B, cap, LANES), jnp.int32),
                   jax.ShapeDtypeStruct((B, cap, LANES), F32)],
        scratch_shapes=[pltpu.VMEM((N_EXPERTS, S), F32),
                        pltpu.VMEM((N_EXPERTS, LANES), jnp.int32),
                        pltpu.SMEM((N_EXPERTS, LANES), jnp.int32),
                        pltpu.SemaphoreType.DMA(())],
        compiler_params=_params(1),
        name="expert_choice_route",
    )(aff_t)


FF_TILE = 256
N_FF_STEPS = D_FF // FF_TILE
FFN_ROW_BLOCK = 1024


def _ffn_kernel(rows_ref, rows_nxt_ref, gate_ref, h2_hbm, wg_hbm, wu_hbm, wd_hbm, xin_hbm, out_hbm,
                xs, yacc, gcol, stage_x, stage_b, wgb, wub, wdb, sem_x, sem_b, sem_s, sem_w, *, n_rows, m_chunk):
    del xin_hbm
    e = pl.program_id(0)
    ne = pl.num_programs(0)
    nf = N_FF_STEPS
    chunk = n_rows // nf
    n_m = n_rows // m_chunk
    per_m = chunk // n_m
    cur = lax.rem(e, 2)

    def x_copy(rows_smem, c, r):
        row = rows_smem[0, 0, c * chunk + r]
        return pltpu.make_async_copy(h2_hbm.at[pl.ds(row, 1)], stage_x.at[pl.ds(r, 1)], sem_x)

    def b_copy(rows_smem, c, r):
        row = rows_smem[0, 0, c * chunk + r]
        return pltpu.make_async_copy(out_hbm.at[pl.ds(row, 1)], stage_b.at[pl.ds(r, 1)], sem_b)

    def s_copy(r):
        row = rows_ref[0, 0, r]
        return pltpu.make_async_copy(yacc.at[pl.ds(r, 1)], out_hbm.at[pl.ds(row, 1)], sem_s)

    def wait_x():
        pltpu.make_async_copy(h2_hbm.at[pl.ds(0, chunk)], stage_x, sem_x).wait()

    def wait_b():
        pltpu.make_async_copy(out_hbm.at[pl.ds(0, chunk)], stage_b, sem_b).wait()

    def wait_s(n):
        pltpu.make_async_copy(yacc.at[pl.ds(0, n)], out_hbm.at[pl.ds(0, n)], sem_s).wait()

    def w_copies(ee, ff, slot):
        fo = pl.multiple_of(ff * FF_TILE, FF_TILE)
        return (pltpu.make_async_copy(wg_hbm.at[ee, :, pl.ds(fo, FF_TILE)], wgb.at[slot], sem_w.at[slot]),
                pltpu.make_async_copy(wu_hbm.at[ee, :, pl.ds(fo, FF_TILE)], wub.at[slot], sem_w.at[slot]),
                pltpu.make_async_copy(wd_hbm.at[ee, pl.ds(fo, FF_TILE), :], wdb.at[slot], sem_w.at[slot]))

    def w_start(ee, ff, slot):
        for cp in w_copies(ee, ff, slot):
            cp.start(priority=1)

    def loop_start(copy_fn, n):
        def body(r, carry):
            copy_fn(r).start()
            return carry
        lax.fori_loop(0, n, body, 0, unroll=8)

    @pl.when(e == 0)
    def _first_expert_rows():
        for c in range(nf - 1):
            loop_start(lambda r, c=c: x_copy(rows_ref, c, r), chunk)
            wait_x()
            xs[0, c * chunk:(c + 1) * chunk, :] = stage_x[...].astype(BF16)
        loop_start(lambda r: x_copy(rows_ref, nf - 1, r), chunk)
        loop_start(lambda r: b_copy(rows_ref, 0, r), chunk)
        w_start(0, 0, 0)

    lane = lax.broadcasted_iota(jnp.int32, (1, 1, LANES), 2)
    g = jnp.sum(jnp.where(lane == e, gate_ref[...], 0.0), axis=-1, keepdims=True)
    gcol[...] = g.reshape(n_rows, 1)

    def split3(rs):
        k = -(-len(rs) // 3)
        return rs[:k], rs[k:2 * k], rs[2 * k:]

    def step(f, first, last):
        pc = lax.rem(f + (nf - 1), nf)
        tgt = lax.rem(e + (0 if first else 1), 2)
        wait_x()
        xs[tgt, pl.ds(pl.multiple_of(pc * chunk, chunk), chunk), :] = stage_x[...].astype(BF16)
        if not first:
            wait_b()
            yacc[pl.ds(pl.multiple_of(f * chunk, chunk), chunk), :] += stage_b[...]

        wslot = lax.rem(f, 2)
        for cp in w_copies(e, f, wslot):
            cp.wait()
        if not last:
            w_start(e, f + 1, 1 - wslot)
        else:
            @pl.when(e < ne - 1)
            def _next_expert_first_tiles():
                w_start(e + 1, 0, 1 - wslot)
        wg = wgb[wslot].astype(BF16)
        wu = wub[wslot].astype(BF16)
        wd = wdb[wslot].astype(BF16)

        x_by_m = [list(range(m * per_m, (m + 1) * per_m)) for m in range(n_m)]
        all_b = list(range(chunk))
        b_tail = []
        if last:
            b_by_m = [[] for _ in range(n_m)]
        elif not first:
            b_by_m = x_by_m
        elif n_m > 1:
            per = -(-chunk // (n_m - 1))
            b_by_m = [[]] + [all_b[i * per:(i + 1) * per] for i in range(n_m - 1)]
        else:
            b_by_m, b_tail = [[]], all_b

        def start_gathers(xr, br):
            for r in xr:
                x_copy(rows_nxt_ref, f, r).start()
            for r in br:
                b_copy(rows_ref, f + 1, r).start(priority=1)

        for m in range(n_m):
            xr, br = split3(x_by_m[m]), split3(b_by_m[m])
            ms = slice(m * m_chunk, (m + 1) * m_chunk)
            start_gathers(xr[0], br[0])
            xm = xs[cur, ms, :]
            a = jnp.dot(xm, wg, preferred_element_type=F32)
            start_gathers(xr[1], br[1])
            u = jnp.dot(xm, wu, preferred_element_type=F32)
            hm = (a * (1.0 / (1.0 + jnp.exp(-a))) * u).astype(BF16)
            start_gathers(xr[2], br[2])
            y = gcol[ms, :] * jnp.dot(hm, wd, preferred_element_type=F32)
            if first:
                yacc[ms, :] = y
                if m == 0:
                    wait_b()
                    yacc[0:chunk, :] += stage_b[...]
                    start_gathers([], b_tail)
            else:
                yacc[ms, :] += y
            if last:
                for r in range(m * m_chunk, (m + 1) * m_chunk):
                    s_copy(r).start()
        if last:
            for m in range(n_m):
                wait_s(m_chunk)

            @pl.when(e < ne - 1)
            def _next_expert_first_residual_chunk():
                loop_start(lambda r: b_copy(rows_nxt_ref, 0, r), chunk)

            @pl.when(e == ne - 1)
            def _drain_unused_prefetch():
                wait_x()

    step(jnp.int32(0), True, False)

    def mid_step(f, carry):
        step(f, False, False)
        return carry
    lax.fori_loop(1, nf - 1, mid_step, 0)
    step(jnp.int32(nf - 1), False, True)


def _expert_ffn(rows, gate_col, h2, w_gate, w_up, w_down, x1):
    T = x1.shape[0]
    n_rows = rows.shape[-1]
    B, cap, _ = gate_col.shape
    chunk = n_rows // N_FF_STEPS
    m_chunk = min(FFN_ROW_BLOCK, n_rows)
    assert chunk % SUBLANES == 0 and chunk % (n_rows // m_chunk) == 0
    rows_spec = lambda f: pl.BlockSpec((1, 1, n_rows), f, memory_space=pltpu.SMEM)
    return pl.pallas_call(
        functools.partial(_ffn_kernel, n_rows=n_rows, m_chunk=m_chunk),
        grid=(N_EXPERTS,),
        in_specs=[rows_spec(lambda e: (e, 0, 0)),
                  rows_spec(lambda e: (jnp.minimum(e + 1, N_EXPERTS - 1), 0, 0)),
                  pl.BlockSpec((B, cap, LANES), lambda e: (0, 0, 0)),
                  pl.BlockSpec(memory_space=pl.ANY),
                  pl.BlockSpec(memory_space=pl.ANY),
                  pl.BlockSpec(memory_space=pl.ANY),
                  pl.BlockSpec(memory_space=pl.ANY),
                  pl.BlockSpec(memory_space=pl.ANY)],
        out_specs=pl.BlockSpec(memory_space=pl.ANY),
        out_shape=jax.ShapeDtypeStruct((T, D_MODEL), F32),
        input_output_aliases={7: 0},
        scratch_shapes=[pltpu.VMEM((2, n_rows, D_MODEL), BF16),
                        pltpu.VMEM((n_rows, D_MODEL), F32),
                        pltpu.VMEM((n_rows, 1), F32),
                        pltpu.VMEM((chunk, D_MODEL), F32),
                        pltpu.VMEM((chunk, D_MODEL), F32),
                        pltpu.VMEM((2, D_MODEL, FF_TILE), F32),
                        pltpu.VMEM((2, D_MODEL, FF_TILE), F32),
                        pltpu.VMEM((2, FF_TILE, D_MODEL), F32),
                        pltpu.SemaphoreType.DMA(()),
                        pltpu.SemaphoreType.DMA(()),
                        pltpu.SemaphoreType.DMA(()),
                        pltpu.SemaphoreType.DMA((2,))],
        compiler_params=_params(1),
        name="expert_ffn",
    )(rows, rows, gate_col, h2, w_gate, w_up, w_down, x1)


def _final_norm_kernel(x_ref, g_ref, o_ref):
    xf = x_ref[...]
    ms = jnp.mean(xf * xf, axis=-1, keepdims=True)
    o_ref[...] = xf * lax.rsqrt(ms + EPS) * g_ref[...]


def _final_norm(x2d, g, tm):
    T = x2d.shape[0]
    return pl.pallas_call(
        _final_norm_kernel,
        grid=(T // tm,),
        in_specs=[pl.BlockSpec((tm, D_MODEL), lambda i: (i, 0)),
                  pl.BlockSpec((1, D_MODEL), lambda i: (0, 0))],
        out_specs=pl.BlockSpec((tm, D_MODEL), lambda i: (i, 0)),
        out_shape=jax.ShapeDtypeStruct((T, D_MODEL), F32),
        compiler_params=_params(1),
        name="final_norm",
    )(x2d, g)


def kernel(x, norm1_g, w_in, pool_w, pool_scale, rel_bias, sink, gn_pool, gn_attn, w_out, norm2_g,
           w_router, w_gate, w_up, w_down, final_g):
    B, S, D = x.shape
    assert D == D_MODEL and S % BLOCK == 0 and norm1_g.shape[0] == 1
    T = B * S
    cap = CAPACITY_FACTOR * S // N_EXPERTS
    tm = min(512, S)
    row = lambda a: a.reshape(1, -1)

    bias_tbl = _bias_table(rel_bias)
    x2d = x.reshape(T, D)
    u, q, kv = _inproj(x2d, row(norm1_g[0]), w_in[0].astype(BF16), tm)
    y_pool = _pool(u.reshape(B, S, POOL_WIDTH), pool_w[0].astype(BF16), row(pool_scale[0]),
                   row(gn_pool[0]), tm)
    y_attn = _attention(q.reshape(B, S, ATTN_WIDTH), kv.reshape(B, S, KV_WIDTH), bias_tbl, sink[0],
                        row(gn_attn[0]))
    x1, h2, aff_t = _outproj(y_pool.reshape(T, POOL_WIDTH), y_attn.reshape(T, ATTN_WIDTH), x2d,
                             w_out[0].astype(BF16), row(norm2_g[0]), w_router[0].T.astype(BF16), tm, B, S)
    rows_col, gate_col = _route(aff_t, cap)
    rows = rows_col[:, :, :N_EXPERTS].transpose(2, 0, 1).reshape(N_EXPERTS, 1, B * cap)
    x2 = _expert_ffn(rows, gate_col, h2, w_gate[0], w_up[0], w_down[0], x1)
    return _final_norm(x2, row(final_g), tm).reshape(B, S, D)
```

```python
import functools
import math

import numpy as np
import jax
import jax.numpy as jnp
from jax import lax
from jax.experimental import pallas as pl
from jax.experimental.pallas import tpu as pltpu

D_MODEL = 2048
POOL_WIDTH = 1024
POOL_WINDOWS = (2, 4, 8, 16)
POOL_GROUP = POOL_WIDTH // len(POOL_WINDOWS)
POOL_HALO = 8
ATTN_WIDTH = 1024
HEAD_DIM = 128
N_HEADS = ATTN_WIDTH // HEAD_DIM
N_KV_HEADS = 2
Q_PER_KV = N_HEADS // N_KV_HEADS
KV_WIDTH = 2 * N_KV_HEADS * HEAD_DIM
WINDOW = 128
BLOCK = 128
N_BUCKETS = 32
MAX_DISTANCE = 128
N_EXPERTS = 16
CAPACITY_FACTOR = 2
D_FF = 2 * D_MODEL
IN_WIDTH = POOL_WIDTH + ATTN_WIDTH + KV_WIDTH
EPS = 1e-6
MASKED = -1e30
LOG2E = math.log2(math.e)
LANES = 128
SUBLANES = 8
VMEM_LIMIT_BYTES = 58 * 1024 * 1024

F32 = jnp.float32
BF16 = jnp.bfloat16


def _params(n_axes):
    return pltpu.CompilerParams(dimension_semantics=("arbitrary",) * n_axes,
                                vmem_limit_bytes=VMEM_LIMIT_BYTES)


def _t5_bucket_table():
    half = N_BUCKETS // 2
    max_exact = half // 2
    qi = np.arange(BLOCK)[:, None]
    kj = np.arange(3 * BLOCK)[None, :] - BLOCK
    rel = kj - qi
    n = np.abs(rel)
    nf = np.maximum(n, 1).astype(np.float64)
    large = max_exact + np.floor(np.log(nf / max_exact) / math.log(MAX_DISTANCE / max_exact)
                                 * (half - max_exact) + 1e-9).astype(np.int64)
    large = np.minimum(large, half - 1)
    bucket = np.where(rel > 0, half, 0) + np.where(n < max_exact, n, large)
    return np.where(n <= WINDOW, bucket, -1).astype(np.int32)


def _bias_kernel(bucket_ref, relb_ref, o_ref):
    bk = bucket_ref[...]
    for h in range(N_HEADS):
        acc = jnp.full(bk.shape, MASKED, F32)
        for b in range(N_BUCKETS):
            acc = jnp.where(bk == b, relb_ref[b, h] * LOG2E, acc)
        o_ref[h] = acc


def _bias_table(rel_bias):
    bucket = jnp.asarray(_t5_bucket_table())
    return pl.pallas_call(
        _bias_kernel,
        out_shape=jax.ShapeDtypeStruct((N_HEADS, BLOCK, 3 * BLOCK), F32),
        in_specs=[pl.BlockSpec(memory_space=pltpu.VMEM), pl.BlockSpec(memory_space=pltpu.SMEM)],
        out_specs=pl.BlockSpec(memory_space=pltpu.VMEM),
        name="bias_table",
    )(bucket, rel_bias)


def _inproj_kernel(x_ref, g_ref, w_ref, u_ref, q_ref, kv_ref, h_scr):
    xf = x_ref[...]
    ms = jnp.mean(xf * xf, axis=-1, keepdims=True)
    h_scr[...] = (xf * lax.rsqrt(ms + EPS) * g_ref[...]).astype(BF16)
    q0, kv0 = POOL_WIDTH, POOL_WIDTH + ATTN_WIDTH
    u_ref[...] = jnp.dot(h_scr[...], w_ref[:, :q0], preferred_element_type=F32)
    q_ref[...] = jnp.dot(h_scr[...], w_ref[:, q0:kv0], preferred_element_type=F32).astype(BF16)
    kv_ref[...] = jnp.dot(h_scr[...], w_ref[:, kv0:], preferred_element_type=F32).astype(BF16)


def _inproj(x2d, g, w_bf16, tm):
    T = x2d.shape[0]
    return pl.pallas_call(
        _inproj_kernel,
        grid=(T // tm,),
        in_specs=[pl.BlockSpec((tm, D_MODEL), lambda i: (i, 0)),
                  pl.BlockSpec((1, D_MODEL), lambda i: (0, 0)),
                  pl.BlockSpec((D_MODEL, IN_WIDTH), lambda i: (0, 0))],
        out_specs=[pl.BlockSpec((tm, POOL_WIDTH), lambda i: (i, 0)),
                   pl.BlockSpec((tm, ATTN_WIDTH), lambda i: (i, 0)),
                   pl.BlockSpec((tm, KV_WIDTH), lambda i: (i, 0))],
        out_shape=[jax.ShapeDtypeStruct((T, POOL_WIDTH), F32),
                   jax.ShapeDtypeStruct((T, ATTN_WIDTH), BF16),
                   jax.ShapeDtypeStruct((T, KV_WIDTH), BF16)],
        scratch_shapes=[pltpu.VMEM((tm, D_MODEL), BF16)],
        compiler_params=_params(1),
        name="norm1_inproj",
    )(x2d, g, w_bf16)


def _pool_kernel(up_ref, uc_ref, un_ref, pw_ref, ps_ref, gn_ref, o_ref, ext, mix, lv_a, lv_b, *, seq):
    s = pl.program_id(1)
    ns = pl.num_programs(1)
    ts = uc_ref.shape[1]
    h = POOL_HALO
    ext[0:h, :] = jnp.where(s > 0, up_ref[0], 0.0)
    ext[h:h + ts, :] = uc_ref[0]
    ext[h + ts:2 * h + ts, :] = jnp.where(s < ns - 1, un_ref[0], 0.0)
    t = s * ts + lax.broadcasted_iota(jnp.int32, (ts, 1), 0)
    for g, w in enumerate(POOL_WINDOWS):
        hw = w // 2
        c0 = g * POOL_GROUP
        def rows(src, start, n):
            return ext[start:start + n, c0:c0 + POOL_GROUP] if src is ext else src[start:start + n, :]
        src, k, n = ext, 1, ts + 2 * h
        for dst in (lv_a, lv_b, lv_a):
            if k >= hw:
                break
            n -= k
            dst[0:n, :] = rows(src, 0, n) + rows(src, k, n)
            src, k = dst, 2 * k
        acc = rows(src, h - hw, ts) + rows(src, h, ts)
        cnt = (jnp.minimum(t + hw, seq) - jnp.maximum(t - hw, 0)).astype(F32)
        pooled = acc / cnt - ext[h:h + ts, c0:c0 + POOL_GROUP]
        m = jnp.dot(pooled.astype(BF16), pw_ref[g], preferred_element_type=F32)
        mix[:, c0:c0 + POOL_GROUP] = m * ps_ref[:, c0:c0 + POOL_GROUP]
    mm = mix[...]
    ms = jnp.mean(mm * mm, axis=-1, keepdims=True)
    o_ref[0] = (mm * lax.rsqrt(ms + EPS) * gn_ref[...]).astype(BF16)


def _pool(u3d, pool_w_bf16, pool_scale, gn_pool, ts):
    B, S, _ = u3d.shape
    hb = ts // POOL_HALO
    n_hblk = S // POOL_HALO
    return pl.pallas_call(
        functools.partial(_pool_kernel, seq=S),
        grid=(B, S // ts),
        in_specs=[pl.BlockSpec((1, POOL_HALO, POOL_WIDTH), lambda b, s: (b, jnp.maximum(s * hb - 1, 0), 0)),
                  pl.BlockSpec((1, ts, POOL_WIDTH), lambda b, s: (b, s, 0)),
                  pl.BlockSpec((1, POOL_HALO, POOL_WIDTH),
                               lambda b, s: (b, jnp.minimum((s + 1) * hb, n_hblk - 1), 0)),
                  pl.BlockSpec((len(POOL_WINDOWS), POOL_GROUP, POOL_GROUP), lambda b, s: (0, 0, 0)),
                  pl.BlockSpec((1, POOL_WIDTH), lambda b, s: (0, 0)),
                  pl.BlockSpec((1, POOL_WIDTH), lambda b, s: (0, 0))],
        out_specs=pl.BlockSpec((1, ts, POOL_WIDTH), lambda b, s: (b, s, 0)),
        out_shape=jax.ShapeDtypeStruct((B, S, POOL_WIDTH), BF16),
        scratch_shapes=[pltpu.VMEM((ts + 2 * POOL_HALO, POOL_WIDTH), F32),
                        pltpu.VMEM((ts, POOL_WIDTH), F32),
                        pltpu.VMEM((ts + 2 * POOL_HALO, POOL_GROUP), F32),
                        pltpu.VMEM((ts + 2 * POOL_HALO, POOL_GROUP), F32)],
        compiler_params=_params(2),
        name="pool_mixer",
    )(u3d, u3d, u3d, pool_w_bf16, pool_scale, gn_pool)


ATTN_BLOCKS_PER_STEP = 4


def _attn_kernel(q_ref, kvp_ref, kvc_ref, kvn_ref, bias_ref, sink_ref, gn_ref, o_ref, att):
    nsub = ATTN_BLOCKS_PER_STEP
    n = pl.program_id(1)
    nb = pl.num_programs(1) * nsub
    scale = HEAD_DIM ** -0.5 * LOG2E
    col = lax.broadcasted_iota(jnp.int32, (1, 3 * BLOCK), 1)
    head = lax.broadcasted_iota(jnp.int32, (Q_PER_KV * BLOCK, 1), 0) // BLOCK

    def kv_block(j, cols):
        if j == 0:
            return kvp_ref[0, :, cols]
        if j == nsub + 1:
            return kvn_ref[0, :, cols]
        return kvc_ref[0, (j - 1) * BLOCK:j * BLOCK, cols]

    for sub in range(nsub):
        blk = n * nsub + sub
        rows = slice(sub * BLOCK, (sub + 1) * BLOCK)
        col_ok = jnp.logical_and(jnp.logical_or(blk > 0, col >= BLOCK),
                                 jnp.logical_or(blk < nb - 1, col < 2 * BLOCK))
        col_mask = jnp.where(col_ok, 0.0, MASKED)
        for k in range(N_KV_HEADS):
            ks = slice(k * HEAD_DIM, (k + 1) * HEAD_DIM)
            vs = slice((N_KV_HEADS + k) * HEAD_DIM, (N_KV_HEADS + k + 1) * HEAD_DIM)
            kk = jnp.concatenate([kv_block(sub + j, ks) for j in range(3)], axis=0)
            vv = jnp.concatenate([kv_block(sub + j, vs) for j in range(3)], axis=0)
            h0 = k * Q_PER_KV
            qk = jnp.concatenate([q_ref[0, rows, (h0 + g) * HEAD_DIM:(h0 + g + 1) * HEAD_DIM]
                                  for g in range(Q_PER_KV)], axis=0)
            logits = lax.dot_general(qk, kk, (((1,), (1,)), ((), ())), preferred_element_type=F32)
            bias = bias_ref[h0:h0 + Q_PER_KV].reshape(Q_PER_KV * BLOCK, 3 * BLOCK)
            logits = logits * scale + bias + col_mask
            sink = jnp.full((Q_PER_KV * BLOCK, 1), sink_ref[h0] * LOG2E, F32)
            for g in range(1, Q_PER_KV):
                sink = jnp.where(head == g, sink_ref[h0 + g] * LOG2E, sink)
            m = jnp.maximum(jnp.max(logits, axis=-1, keepdims=True), sink)
            p = jnp.exp2(logits - m)
            denom = jnp.sum(p, axis=-1, keepdims=True) + jnp.exp2(sink - m)
            o = jnp.dot(p.astype(BF16), vv, preferred_element_type=F32) / denom
            for g in range(Q_PER_KV):
                att[rows, (h0 + g) * HEAD_DIM:(h0 + g + 1) * HEAD_DIM] = o[g * BLOCK:(g + 1) * BLOCK]
    a = att[...]
    ms = jnp.mean(a * a, axis=-1, keepdims=True)
    o_ref[0] = (a * lax.rsqrt(ms + EPS) * gn_ref[...]).astype(BF16)


def _attention(q3d, kv3d, bias_tbl, sink, gn_attn):
    B, S, _ = q3d.shape
    nb = S // BLOCK
    nsub = ATTN_BLOCKS_PER_STEP
    assert nb % nsub == 0
    rows = nsub * BLOCK
    kv_spec = lambda f: pl.BlockSpec((1, BLOCK, KV_WIDTH), f)
    return pl.pallas_call(
        _attn_kernel,
        grid=(B, nb // nsub),
        in_specs=[pl.BlockSpec((1, rows, ATTN_WIDTH), lambda b, n: (b, n, 0)),
                  kv_spec(lambda b, n: (b, jnp.maximum(n * nsub - 1, 0), 0)),
                  pl.BlockSpec((1, rows, KV_WIDTH), lambda b, n: (b, n, 0)),
                  kv_spec(lambda b, n: (b, jnp.minimum((n + 1) * nsub, nb - 1), 0)),
                  pl.BlockSpec((N_HEADS, BLOCK, 3 * BLOCK), lambda b, n: (0, 0, 0)),
                  pl.BlockSpec(memory_space=pltpu.SMEM),
                  pl.BlockSpec((1, ATTN_WIDTH), lambda b, n: (0, 0))],
        out_specs=pl.BlockSpec((1, rows, ATTN_WIDTH), lambda b, n: (b, n, 0)),
        out_shape=jax.ShapeDtypeStruct((B, S, ATTN_WIDTH), BF16),
        scratch_shapes=[pltpu.VMEM((rows, ATTN_WIDTH), F32)],
        compiler_params=_params(2),
        name="window_attn",
    )(q3d, kv3d, kv3d, kv3d, bias_tbl, sink, gn_attn)


OUT_CHUNK = 1024


def _outproj_kernel(yp_ref, ya_ref, x_ref, wo_ref, g2_ref, wrt_ref, x1_ref, h2_ref, aff_ref, mix):
    tm = x_ref.shape[0]
    mix[:, 0:POOL_WIDTH] = yp_ref[...]
    mix[:, POOL_WIDTH:] = ya_ref[...]
    ss = jnp.zeros((tm, 1), F32)
    for n0 in range(0, D_MODEL, OUT_CHUNK):
        ns = slice(n0, n0 + OUT_CHUNK)
        x1 = x_ref[:, ns] + jnp.dot(mix[...], wo_ref[:, ns], preferred_element_type=F32)
        x1_ref[:, ns] = x1
        ss = ss + jnp.sum(x1 * x1, axis=-1, keepdims=True)
    inv = lax.rsqrt(ss / D_MODEL + EPS)
    h2 = x1_ref[...] * inv * g2_ref[...]
    h2_ref[...] = h2
    logits_t = lax.dot_general(wrt_ref[...], h2.astype(BF16), (((1,), (1,)), ((), ())),
                               preferred_element_type=F32)
    mx = jnp.max(logits_t, axis=0, keepdims=True)
    ex = jnp.exp(logits_t - mx)
    aff_ref[0] = ex / jnp.sum(ex, axis=0, keepdims=True)


def _outproj(yp2d, ya2d, x2d, wo_bf16, g2, wr_t, tm, B, S):
    T = x2d.shape[0]
    per_b = S // tm
    return pl.pallas_call(
        _outproj_kernel,
        grid=(T // tm,),
        in_specs=[pl.BlockSpec((tm, POOL_WIDTH), lambda i: (i, 0)),
                  pl.BlockSpec((tm, ATTN_WIDTH), lambda i: (i, 0)),
                  pl.BlockSpec((tm, D_MODEL), lambda i: (i, 0)),
                  pl.BlockSpec((D_MODEL, D_MODEL), lambda i: (0, 0)),
                  pl.BlockSpec((1, D_MODEL), lambda i: (0, 0)),
                  pl.BlockSpec((N_EXPERTS, D_MODEL), lambda i: (0, 0))],
        out_specs=[pl.BlockSpec((tm, D_MODEL), lambda i: (i, 0)),
                   pl.BlockSpec((tm, D_MODEL), lambda i: (i, 0)),
                   pl.BlockSpec((1, N_EXPERTS, tm), lambda i: (i // per_b, 0, i % per_b))],
        out_shape=[jax.ShapeDtypeStruct((T, D_MODEL), F32),
                   jax.ShapeDtypeStruct((T, D_MODEL), F32),
                   jax.ShapeDtypeStruct((B, N_EXPERTS, S), F32)],
        scratch_shapes=[pltpu.VMEM((tm, POOL_WIDTH + ATTN_WIDTH), BF16)],
        compiler_params=_params(1),
        name="outproj_norm2_router",
    )(yp2d, ya2d, x2d, wo_bf16, g2, wr_t)


def _route_kernel(aff_ref, rows_ref, gate_ref, posm, bounds_vmem, bounds_smem, bounds_sem, *, cap, slot_blk):
    n_lists, S = aff_ref.shape
    v = aff_ref[...]
    bits = pltpu.bitcast(v, jnp.int32)

    def count(mask):
        return jnp.sum(jnp.where(mask, 1.0, 0.0), axis=1, keepdims=True)

    thr = jnp.zeros((n_lists, 1), jnp.int32)
    for bit in range(30, -1, -1):
        cand = thr | (1 << bit)
        thr = jnp.where(count(bits >= cand) >= cap, cand, thr)
    gt = bits > thr
    eq = bits == thr
    need = cap - count(gt)
    tok = lax.broadcasted_iota(jnp.int32, (n_lists, S), 1)
    bound = jnp.zeros((n_lists, 1), jnp.int32)
    for bit in range(S.bit_length() - 1, -1, -1):
        cand = bound | (1 << bit)
        bound = jnp.where(count(jnp.logical_and(eq, tok < cand)) <= need, cand, bound)
    sel = jnp.where(jnp.logical_or(gt, jnp.logical_and(eq, tok < bound)), 1.0, 0.0)

    tri = jnp.where(lax.broadcasted_iota(jnp.int32, (LANES, LANES), 0)
                    < lax.broadcasted_iota(jnp.int32, (LANES, LANES), 1), 1.0, 0.0).astype(BF16)
    lane = lax.broadcasted_iota(jnp.int32, (1, LANES), 1)
    n_chunks = S // LANES
    assert n_chunks <= LANES and 2 * (cap // slot_blk) <= LANES
    off = jnp.zeros((n_lists, 1), F32)
    slots_before = jnp.full((n_lists, LANES), float(S), F32)
    slots_through = jnp.full((n_lists, LANES), float(S), F32)
    for c in range(n_chunks):
        sc = sel[:, c * LANES:(c + 1) * LANES]
        within = jnp.dot(sc.astype(BF16), tri, preferred_element_type=F32)
        posm[:, c * LANES:(c + 1) * LANES] = jnp.where(sc > 0.0, within + off, -1.0)
        slots_before = jnp.where(lane == c, off, slots_before)
        off = off + jnp.sum(sc, axis=1, keepdims=True)
        slots_through = jnp.where(lane == c, off, slots_through)

    bounds = jnp.zeros((n_lists, LANES), jnp.int32)
    for jb in range(cap // slot_blk):
        first = count(slots_through <= float(jb * slot_blk)).astype(jnp.int32)
        last = count(slots_before < float((jb + 1) * slot_blk)).astype(jnp.int32)
        bounds = jnp.where(lane == 2 * jb, first, bounds)
        bounds = jnp.where(lane == 2 * jb + 1, last, bounds)
    bounds_vmem[...] = bounds
    to_smem = pltpu.make_async_copy(bounds_vmem, bounds_smem, bounds_sem)
    to_smem.start()
    to_smem.wait()

    rows_ref[...] = jnp.zeros(rows_ref.shape, jnp.int32)
    gate_ref[...] = jnp.zeros(gate_ref.shape, F32)
    for r in range(n_lists):
        b, e = divmod(r, N_EXPERTS)
        for jb in range(cap // slot_blk):
            slot = (lax.broadcasted_iota(jnp.int32, (slot_blk, 1), 0) + jb * slot_blk).astype(F32)

            def body(c, carry):
                acc_row, acc_gate = carry
                o = pl.multiple_of(c * LANES, LANES)
                hit = posm[r:r + 1, pl.ds(o, LANES)] == slot
                rowid = (lane + (c * LANES + b * S)).astype(F32)
                acc_row = jnp.where(hit, rowid, acc_row)
                acc_gate = jnp.where(hit, aff_ref[r:r + 1, pl.ds(o, LANES)], acc_gate)
                return acc_row, acc_gate

            zero = jnp.zeros((slot_blk, LANES), F32)
            acc_row, acc_gate = lax.fori_loop(bounds_smem[r, 2 * jb], bounds_smem[r, 2 * jb + 1], body,
                                              (zero, zero))
            js = slice(jb * slot_blk, (jb + 1) * slot_blk)
            rows_ref[b, js, e:e + 1] = jnp.sum(acc_row, axis=1, keepdims=True).astype(jnp.int32)
            gate_ref[b, js, e:e + 1] = jnp.sum(acc_gate, axis=1, keepdims=True)


def _route(aff_t, cap):
    B, _, S = aff_t.shape
    slot_blk = min(LANES, cap)
    return pl.pallas_call(
        functools.partial(_route_kernel, cap=cap, slot_blk=slot_blk),
        in_specs=[pl.BlockSpec(memory_space=pltpu.VMEM)],
        out_specs=[pl.BlockSpec(memory_space=pltpu.VMEM), pl.BlockSpec(memory_space=pltpu.VMEM)],
        out_shape=[jax.ShapeDtypeStruct((B, cap, LANES), jnp.int32),
                   jax.ShapeDtypeStruct((B, cap, LANES), F32)],
        scratch_shapes=[pltpu.VMEM((B * N_EXPERTS, S), F32),
                        pltpu.VMEM((B * N_EXPERTS, LANES), jnp.int32),
                        pltpu.SMEM((B * N_EXPERTS, LANES), jnp.int32),
                        pltpu.SemaphoreType.DMA(())],
        compiler_params=_params(0),
        name="expert_choice_route",
    )(aff_t.reshape(B * N_EXPERTS, S))


FF_TILE = 256
N_FF_STEPS = D_FF // FF_TILE
FFN_ROW_BLOCK = 1024


def _ffn_kernel(rows_ref, rows_nxt_ref, gate_ref, h2_hbm, wg_hbm, wu_hbm, wd_hbm, xin_hbm, out_hbm,
                xs, yacc, gcol, stage_x, stage_b, wgb, wub, wdb, sem_x, sem_b, sem_s, sem_w, *, n_rows, m_chunk):
    del xin_hbm
    e = pl.program_id(0)
    ne = pl.num_programs(0)
    nf = N_FF_STEPS
    chunk = n_rows // nf
    n_m = n_rows // m_chunk
    per_m = chunk // n_m
    cur = lax.rem(e, 2)

    def x_copy(rows_smem, c, r):
        row = rows_smem[0, 0, c * chunk + r]
        return pltpu.make_async_copy(h2_hbm.at[pl.ds(row, 1)], stage_x.at[pl.ds(r, 1)], sem_x)

    def b_copy(rows_smem, c, r):
        row = rows_smem[0, 0, c * chunk + r]
        return pltpu.make_async_copy(out_hbm.at[pl.ds(row, 1)], stage_b.at[pl.ds(r, 1)], sem_b)

    def s_copy(r):
        row = rows_ref[0, 0, r]
        return pltpu.make_async_copy(yacc.at[pl.ds(r, 1)], out_hbm.at[pl.ds(row, 1)], sem_s)

    def wait_x():
        pltpu.make_async_copy(h2_hbm.at[pl.ds(0, chunk)], stage_x, sem_x).wait()

    def wait_b():
        pltpu.make_async_copy(out_hbm.at[pl.ds(0, chunk)], stage_b, sem_b).wait()

    def wait_s(n):
        pltpu.make_async_copy(yacc.at[pl.ds(0, n)], out_hbm.at[pl.ds(0, n)], sem_s).wait()

    def w_copies(ee, ff, slot):
        fo = pl.multiple_of(ff * FF_TILE, FF_TILE)
        return (pltpu.make_async_copy(wg_hbm.at[ee, :, pl.ds(fo, FF_TILE)], wgb.at[slot], sem_w.at[slot]),
                pltpu.make_async_copy(wu_hbm.at[ee, :, pl.ds(fo, FF_TILE)], wub.at[slot], sem_w.at[slot]),
                pltpu.make_async_copy(wd_hbm.at[ee, pl.ds(fo, FF_TILE), :], wdb.at[slot], sem_w.at[slot]))

    def w_start(ee, ff, slot):
        for cp in w_copies(ee, ff, slot):
            cp.start(priority=1)

    def loop_start(copy_fn, n):
        def body(r, carry):
            copy_fn(r).start()
            return carry
        lax.fori_loop(0, n, body, 0, unroll=8)

    @pl.when(e == 0)
    def _first_expert_rows():
        for c in range(nf - 1):
            loop_start(lambda r, c=c: x_copy(rows_ref, c, r), chunk)
            wait_x()
            xs[0, c * chunk:(c + 1) * chunk, :] = stage_x[...].astype(BF16)
        loop_start(lambda r: x_copy(rows_ref, nf - 1, r), chunk)
        loop_start(lambda r: b_copy(rows_ref, 0, r), chunk)
        w_start(0, 0, 0)

    lane = lax.broadcasted_iota(jnp.int32, (1, 1, LANES), 2)
    g = jnp.sum(jnp.where(lane == e, gate_ref[...], 0.0), axis=-1, keepdims=True)
    gcol[...] = g.reshape(n_rows, 1)

    def split3(rs):
        k = -(-len(rs) // 3)
        return rs[:k], rs[k:2 * k], rs[2 * k:]

    def step(f, first, last):
        pc = lax.rem(f + (nf - 1), nf)
        tgt = lax.rem(e + (0 if first else 1), 2)
        wait_x()
        xs[tgt, pl.ds(pl.multiple_of(pc * chunk, chunk), chunk), :] = stage_x[...].astype(BF16)
        if not first:
            wait_b()
            yacc[pl.ds(pl.multiple_of(f * chunk, chunk), chunk), :] += stage_b[...]

        wslot = lax.rem(f, 2)
        for cp in w_copies(e, f, wslot):
            cp.wait()
        if not last:
            w_start(e, f + 1, 1 - wslot)
        else:
            @pl.when(e < ne - 1)
            def _next_expert_first_tiles():
                w_start(e + 1, 0, 1 - wslot)
        wg = wgb[wslot].astype(BF16)
        wu = wub[wslot].astype(BF16)
        wd = wdb[wslot].astype(BF16)

        x_by_m = [list(range(m * per_m, (m + 1) * per_m)) for m in range(n_m)]
        all_b = list(range(chunk))
        b_tail = []
        if last:
            b_by_m = [[] for _ in range(n_m)]
        elif not first:
            b_by_m = x_by_m
        elif n_m > 1:
            per = -(-chunk // (n_m - 1))
            b_by_m = [[]] + [all_b[i * per:(i + 1) * per] for i in range(n_m - 1)]
        else:
            b_by_m, b_tail = [[]], all_b

        def start_gathers(xr, br):
            for r in xr:
                x_copy(rows_nxt_ref, f, r).start()
            for r in br:
                b_copy(rows_ref, f + 1, r).start(priority=1)

        for m in range(n_m):
            xr, br = split3(x_by_m[m]), split3(b_by_m[m])
            ms = slice(m * m_chunk, (m + 1) * m_chunk)
            start_gathers(xr[0], br[0])
            xm = xs[cur, ms, :]
            a = jnp.dot(xm, wg, preferred_element_type=F32)
            start_gathers(xr[1], br[1])
            u = jnp.dot(xm, wu, preferred_element_type=F32)
            hm = (a * (1.0 / (1.0 + jnp.exp(-a))) * u).astype(BF16)
            start_gathers(xr[2], br[2])
            y = gcol[ms, :] * jnp.dot(hm, wd, preferred_element_type=F32)
            if first:
                yacc[ms, :] = y
                if m == 0:
                    wait_b()
                    yacc[0:chunk, :] += stage_b[...]
                    start_gathers([], b_tail)
            else:
                yacc[ms, :] += y
            if last:
                for r in range(m * m_chunk, (m + 1) * m_chunk):
                    s_copy(r).start()
        if last:
            for m in range(n_m):
                wait_s(m_chunk)

            @pl.when(e < ne - 1)
            def _next_expert_first_residual_chunk():
                loop_start(lambda r: b_copy(rows_nxt_ref, 0, r), chunk)

            @pl.when(e == ne - 1)
            def _drain_unused_prefetch():
                wait_x()

    step(jnp.int32(0), True, False)

    def mid_step(f, carry):
        step(f, False, False)
        return carry
    lax.fori_loop(1, nf - 1, mid_step, 0)
    step(jnp.int32(nf - 1), False, True)


def _expert_ffn(rows, gate_col, h2, w_gate, w_up, w_down, x1):
    T = x1.shape[0]
    n_rows = rows.shape[-1]
    B, cap, _ = gate_col.shape
    chunk = n_rows // N_FF_STEPS
    m_chunk = min(FFN_ROW_BLOCK, n_rows)
    assert chunk % SUBLANES == 0 and chunk % (n_rows // m_chunk) == 0
    rows_spec = lambda f: pl.BlockSpec((1, 1, n_rows), f, memory_space=pltpu.SMEM)
    return pl.pallas_call(
        functools.partial(_ffn_kernel, n_rows=n_rows, m_chunk=m_chunk),
        grid=(N_EXPERTS,),
        in_specs=[rows_spec(lambda e: (e, 0, 0)),
                  rows_spec(lambda e: (jnp.minimum(e + 1, N_EXPERTS - 1), 0, 0)),
                  pl.BlockSpec((B, cap, LANES), lambda e: (0, 0, 0)),
                  pl.BlockSpec(memory_space=pl.ANY),
                  pl.BlockSpec(memory_space=pl.ANY),
                  pl.BlockSpec(memory_space=pl.ANY),
                  pl.BlockSpec(memory_space=pl.ANY),
                  pl.BlockSpec(memory_space=pl.ANY)],
        out_specs=pl.BlockSpec(memory_space=pl.ANY),
        out_shape=jax.ShapeDtypeStruct((T, D_MODEL), F32),
        input_output_aliases={7: 0},
        scratch_shapes=[pltpu.VMEM((2, n_rows, D_MODEL), BF16),
                        pltpu.VMEM((n_rows, D_MODEL), F32),
                        pltpu.VMEM((n_rows, 1), F32),
                        pltpu.VMEM((chunk, D_MODEL), F32),
                        pltpu.VMEM((chunk, D_MODEL), F32),
                        pltpu.VMEM((2, D_MODEL, FF_TILE), F32),
                        pltpu.VMEM((2, D_MODEL, FF_TILE), F32),
                        pltpu.VMEM((2, FF_TILE, D_MODEL), F32),
                        pltpu.SemaphoreType.DMA(()),
                        pltpu.SemaphoreType.DMA(()),
                        pltpu.SemaphoreType.DMA(()),
                        pltpu.SemaphoreType.DMA((2,))],
        compiler_params=_params(1),
        name="expert_ffn",
    )(rows, rows, gate_col, h2, w_gate, w_up, w_down, x1)


def _final_norm_kernel(x_ref, g_ref, o_ref):
    xf = x_ref[...]
    ms = jnp.mean(xf * xf, axis=-1, keepdims=True)
    o_ref[...] = xf * lax.rsqrt(ms + EPS) * g_ref[...]


def _final_norm(x2d, g, tm):
    T = x2d.shape[0]
    return pl.pallas_call(
        _final_norm_kernel,
        grid=(T // tm,),
        in_specs=[pl.BlockSpec((tm, D_MODEL), lambda i: (i, 0)),
                  pl.BlockSpec((1, D_MODEL), lambda i: (0, 0))],
        out_specs=pl.BlockSpec((tm, D_MODEL), lambda i: (i, 0)),
        out_shape=jax.ShapeDtypeStruct((T, D_MODEL), F32),
        compiler_params=_params(1),
        name="final_norm",
    )(x2d, g)


def kernel(x, norm1_g, w_in, pool_w, pool_scale, rel_bias, sink, gn_pool, gn_attn, w_out, norm2_g,
           w_router, w_gate, w_up, w_down, final_g):
    B, S, D = x.shape
    assert D == D_MODEL and S % BLOCK == 0 and norm1_g.shape[0] == 1
    T = B * S
    cap = CAPACITY_FACTOR * S // N_EXPERTS
    tm = min(512, S)
    row = lambda a: a.reshape(1, -1)

    bias_tbl = _bias_table(rel_bias)
    x2d = x.reshape(T, D)
    u, q, kv = _inproj(x2d, row(norm1_g[0]), w_in[0].astype(BF16), tm)
    y_pool = _pool(u.reshape(B, S, POOL_WIDTH), pool_w[0].astype(BF16), row(pool_scale[0]),
                   row(gn_pool[0]), tm)
    y_attn = _attention(q.reshape(B, S, ATTN_WIDTH), kv.reshape(B, S, KV_WIDTH), bias_tbl, sink[0],
                        row(gn_attn[0]))
    x1, h2, aff_t = _outproj(y_pool.reshape(T, POOL_WIDTH), y_attn.reshape(T, ATTN_WIDTH), x2d,
                             w_out[0].astype(BF16), row(norm2_g[0]), w_router[0].T.astype(BF16), tm, B, S)
    rows_col, gate_col = _route(aff_t, cap)
    rows = rows_col[:, :, :N_EXPERTS].transpose(2, 0, 1).reshape(N_EXPERTS, 1, B * cap)
    x2 = _expert_ffn(rows, gate_col, h2, w_gate[0], w_up[0], w_down[0], x1)
    return _final_norm(x2, row(final_g), tm).reshape(B, S, D)
```
